```python
import math
import jax
import jax.numpy as jnp
from jax import lax
import numpy as np

D_MODEL = 2048
BATCH = 2
SEQ = 4096
DEPTH = 4
DEC_BATCH = 8
DEC_SEQ = 4
PAST_LEN = 16384
PAGE_SIZE = 128

GROUP = D_MODEL // 4
M_HEADS = 4
M_DV = GROUP // M_HEADS
M_DK = M_DV // 2
M_CHUNK = 64
GATE_CAP = 15.0
R_HEAD = 64
R_HEADS = GROUP // R_HEAD
R_LORA_W = max(32, int(round(GROUP ** 0.5 * 1.8 / 32)) * 32)
R_LORA_A = max(32, int(round(GROUP ** 0.5 * 1.8 / 32)) * 32)
R_LORA_G = max(32, int(round(GROUP ** 0.6 * 0.6 / 32)) * 32)
R_GN_EPS = 64e-5
G_HEADS = 4
G_DV = GROUP // G_HEADS
G_DK = G_DV // 2
G_LORA = 16
G_TAU = 16.0
G_CHUNK = 64
A_HEAD = 64
A_HEADS = GROUP // A_HEAD
A_ROT = A_HEAD // 4
ROPE_THETA = 500000.0
MOBA_BLOCK = 256
MOBA_TOPK = 3
MOBA_QCHUNK = 32
D_FF = ((8 * D_MODEL // 3 + 255) // 256) * 256
RMS_EPS = 1e-6
P_M = 2 * M_HEADS * M_DK + 2 * GROUP + 2 * M_HEADS
P_R = 3 * GROUP + R_LORA_W + R_LORA_A + R_LORA_G
P_G = 2 * G_HEADS * G_DK + 2 * GROUP + G_LORA
P_A = 3 * GROUP
P_TOTAL = P_M + P_R + P_G + P_A

kernel_name = 'hybrid_mlstm_rwkv7_gla_moba_macaron_step'

F32 = jnp.float32


def rmsnorm(x, w, eps=RMS_EPS):
    xf = x.astype(F32)
    y = xf * lax.rsqrt(jnp.mean(xf * xf, axis=-1, keepdims=True) + eps)
    return (y * w.astype(F32)).astype(x.dtype)


def swiglu(x, w_in, w_out):
    gate, up = jnp.split(x @ w_in, 2, axis=-1)
    return (jax.nn.silu(gate) * up) @ w_out


def partial_rope(x, pos):
    half = A_ROT // 2
    inv = ROPE_THETA ** (-(jnp.arange(half, dtype=F32) * 2.0 / A_ROT))
    ang = pos.astype(F32)[:, None] * inv[None, :]
    cos = jnp.cos(ang)[None, :, None, :]
    sin = jnp.sin(ang)[None, :, None, :]
    xf = x.astype(F32)
    x1, x2 = xf[..., :half], xf[..., half:A_ROT]
    out = jnp.concatenate([x1 * cos - x2 * sin, x2 * cos + x1 * sin, xf[..., A_ROT:]], axis=-1)
    return out.astype(x.dtype)


def mlstm_chunked(q, k, v, ig, lf, c0, n0, m0):
    B, H, T, _ = q.shape
    L = math.gcd(T, M_CHUNK)
    nc = T // L
    causal = jnp.tril(jnp.ones((L, L), bool))

    def to_chunks(a):
        return jnp.moveaxis(a.reshape(B, H, nc, L, *a.shape[3:]), 2, 0)

    def step(carry, inp):
        c, n, m = carry
        qc, kc, vc, ic, fc = inp
        b = jnp.cumsum(fc, axis=-1)
        dmat = b[..., :, None] - b[..., None, :] + ic[..., None, :]
        dmat = jnp.where(causal, dmat, -jnp.inf)
        m_inter = b + m[..., None]
        mt = jnp.maximum(m_inter, jnp.max(dmat, axis=-1))
        s_inter = jnp.exp(m_inter - mt)
        w = jnp.exp(dmat - mt[..., None]) * jnp.einsum('bhtd,bhsd->bhts', qc, kc)
        num = s_inter[..., None] * jnp.einsum('bhtd,bhde->bhte', qc, c) + jnp.einsum('bhts,bhse->bhte', w, vc)
        den = s_inter * jnp.einsum('bhtd,bhd->bht', qc, n) + jnp.sum(w, axis=-1)
        h = num / jnp.maximum(jnp.abs(den), jnp.exp(-mt))[..., None]
        m_new = mt[..., -1]
        carry_scale = jnp.exp(b[..., -1] + m - m_new)
        ws = jnp.exp(b[..., -1:] - b + ic - m_new[..., None])
        c_new = carry_scale[..., None, None] * c + jnp.einsum('bhs,bhsd,bhse->bhde', ws, kc, vc)
        n_new = carry_scale[..., None] * n + jnp.einsum('bhs,bhsd->bhd', ws, kc)
        return (c_new, n_new, m_new), h

    (c, n, m), hs = lax.scan(step, (c0, n0, m0), tuple(to_chunks(a) for a in (q, k, v, ig, lf)))
    h = jnp.moveaxis(hs, 0, 2).reshape(B, H, T, -1)
    return h, c, n, m


def gla_chunked(q, k, v, la, s0):
    B, H, T, _ = q.shape
    L = math.gcd(T, G_CHUNK)
    nc = T // L
    causal = jnp.tril(jnp.ones((L, L), bool))

    def to_chunks(a):
        return jnp.moveaxis(a.reshape(B, H, nc, L, a.shape[-1]), 2, 0)

    def step(s, inp):
        qc, kc, vc, ac = inp
        b = jnp.cumsum(ac, axis=2)
        diff = b[:, :, :, None, :] - b[:, :, None, :, :]
        diff = jnp.where(causal[:, :, None], diff, -jnp.inf)
        att = jnp.einsum('bhtd,bhsd,bhtsd->bhts', qc, kc, jnp.exp(diff))
        o = jnp.einsum('bhtd,bhde->bhte', qc * jnp.exp(b), s) + jnp.einsum('bhts,bhse->bhte', att, vc)
        bl = b[:, :, -1]
        s_new = jnp.exp(bl)[..., None] * s + jnp.einsum('bhsd,bhse->bhde', kc * jnp.exp(bl[:, :, None] - b), vc)
        return s_new, o

    s, os_ = lax.scan(step, s0, tuple(to_chunks(a) for a in (q, k, v, la)))
    return jnp.moveaxis(os_, 0, 2).reshape(B, H, T, -1), s


def rwkv7_scan(r, w, k, v, a, b, s0):
    def step(s, inp):
        rt, wt, kt, vt, at, bt = inp
        sa = jnp.einsum('bhvk,bhk->bhv', s, at)
        s = s * wt[:, :, None, :] + sa[..., None] * bt[:, :, None, :] + vt[..., None] * kt[:, :, None, :]
        return s, jnp.einsum('bhvk,bhk->bhv', s, rt)

    s, ys = lax.scan(step, s0, tuple(jnp.moveaxis(t, 1, 0) for t in (r, w, k, v, a, b)))
    return s, jnp.moveaxis(ys, 0, 1)


def moba_attention(q, k, v, q_start):
    B, H, Tq, hd = q.shape
    Tk = k.shape[2]
    nb = -(-Tk // MOBA_BLOCK)
    pad = nb * MOBA_BLOCK - Tk
    padw = ((0, 0), (0, 0), (0, pad), (0, 0))
    kb = jnp.pad(k.astype(F32), padw).reshape(B, H, nb, MOBA_BLOCK, hd)
    vb = jnp.pad(v.astype(F32), padw).reshape(B, H, nb, MOBA_BLOCK, hd)
    kmean = jnp.mean(kb, axis=3)
    n_sel = min(MOBA_TOPK, nb)
    qc = math.gcd(Tq, MOBA_QCHUNK)
    bi = jnp.arange(B)[:, None, None, None]
    hi = jnp.arange(H)[None, :, None, None]
    scale = hd ** -0.5
    qf = q.astype(F32)

    def one_chunk(c):
        start = q_start + c * qc
        own = start // MOBA_BLOCK
        pos = start + jnp.arange(qc)
        qq = lax.dynamic_slice_in_dim(qf, c * qc, qc, axis=2)
        gate = jnp.einsum('bhqd,bhnd->bhqn', qq, kmean)
        gate = jnp.where(jnp.arange(nb) < own, gate, -jnp.inf)
        _, sel = lax.top_k(gate, n_sel)
        valid = jnp.arange(n_sel) < own
        k_sel = kb[bi, hi, sel]
        v_sel = vb[bi, hi, sel]
        s_sel = jnp.einsum('bhqd,bhqskd->bhqsk', qq, k_sel) * scale
        s_sel = jnp.where(valid[:, None], s_sel, -jnp.inf).reshape(B, H, qc, n_sel * MOBA_BLOCK)
        k_own = lax.dynamic_index_in_dim(kb, own, axis=2, keepdims=False)
        v_own = lax.dynamic_index_in_dim(vb, own, axis=2, keepdims=False)
        s_own = jnp.einsum('bhqd,bhkd->bhqk', qq, k_own) * scale
        kpos = own * MOBA_BLOCK + jnp.arange(MOBA_BLOCK)
        s_own = jnp.where(kpos[None, :] <= pos[:, None], s_own, -jnp.inf)
        p = jax.nn.softmax(jnp.concatenate([s_sel, s_own], axis=-1), axis=-1)
        p_sel = p[..., :n_sel * MOBA_BLOCK].reshape(B, H, qc, n_sel, MOBA_BLOCK)
        return (jnp.einsum('bhqsk,bhqskd->bhqd', p_sel, v_sel)
                + jnp.einsum('bhqk,bhkd->bhqd', p[..., n_sel * MOBA_BLOCK:], v_own))

    out = lax.map(one_chunk, jnp.arange(Tq // qc))
    return jnp.moveaxis(out, 0, 2).reshape(B, H, Tq, hd)


def mlstm_group(pm, c0, n0, m0, gate_b, norm_w):
    B, T, _ = pm.shape
    qk = M_HEADS * M_DK
    pf = pm.astype(F32)
    q, k, v, o, gates = jnp.split(pf, [qk, 2 * qk, 2 * qk + GROUP, 2 * qk + 2 * GROUP], axis=-1)

    def heads(a, d):
        return a.reshape(B, T, M_HEADS, d).transpose(0, 2, 1, 3)

    g = gates + gate_b.astype(F32)
    g = GATE_CAP * jnp.tanh(g / GATE_CAP)
    ig = g[..., :M_HEADS].transpose(0, 2, 1)
    lf = jax.nn.log_sigmoid(g[..., M_HEADS:]).transpose(0, 2, 1)
    h, c, n, m = mlstm_chunked(heads(q, M_DK), heads(k, M_DK) * (M_DK ** -0.5), heads(v, M_DV),
                               ig, lf, c0.astype(F32), n0.astype(F32), m0.astype(F32))
    h = rmsnorm(h.transpose(0, 2, 1, 3), norm_w.reshape(M_HEADS, M_DV))
    y = h.reshape(B, T, GROUP) * jax.nn.sigmoid(o)
    return y.astype(pm.dtype), c, n, m


def rwkv_group(pr, s0, shift0, mu, w0, w2, a0, a2, g2, k_k, k_a, r_k, ln_w, ln_b):
    B, T, _ = pr.shape
    pf = pr.astype(F32)
    prev = jnp.concatenate([shift0.astype(F32)[:, None], pf[:, :-1]], axis=1)
    xs = pf + (prev - pf) * mu.astype(F32)
    r, k, v, xw, xa, xg = jnp.split(
        xs, [GROUP, 2 * GROUP, 3 * GROUP, 3 * GROUP + R_LORA_W, 3 * GROUP + R_LORA_W + R_LORA_A], axis=-1)
    w = -jax.nn.softplus(-(w0.astype(F32) + jnp.tanh(xw) @ w2.astype(F32))) - 0.5
    decay = jnp.exp(-jnp.exp(w))
    a = jax.nn.sigmoid(a0.astype(F32) + xa @ a2.astype(F32))
    g = jax.nn.sigmoid(xg) @ g2.astype(F32)

    def hs(t):
        return t.reshape(B, T, R_HEADS, R_HEAD)

    kk = hs(k * k_k.astype(F32))
    kk = kk / jnp.maximum(jnp.sqrt(jnp.sum(kk * kk, axis=-1, keepdims=True)), 1e-12)
    k = k * (1.0 + (a - 1.0) * k_a.astype(F32))
    r_h, k_h, v_h = hs(r), hs(k), hs(v)
    s, y = rwkv7_scan(r_h, hs(decay), k_h, v_h, -kk, kk * hs(a), s0.astype(F32))
    mean = jnp.mean(y, axis=-1, keepdims=True)
    var = jnp.mean(jnp.square(y - mean), axis=-1, keepdims=True)
    y = ((y - mean) * lax.rsqrt(var + R_GN_EPS)).reshape(B, T, GROUP) * ln_w.astype(F32) + ln_b.astype(F32)
    y = y + (jnp.sum(r_h * k_h * r_k.astype(F32), axis=-1, keepdims=True) * v_h).reshape(B, T, GROUP)
    return (y * g).astype(pr.dtype), s, pf[:, -1]


def gla_group(pg, s0, gk_up, gk_b, norm_w):
    B, T, _ = pg.shape
    kd = G_HEADS * G_DK
    pf = pg.astype(F32)
    q, k, v, gk, g = jnp.split(pf, [kd, 2 * kd, 2 * kd + GROUP, 2 * kd + GROUP + G_LORA], axis=-1)
    la = jax.nn.log_sigmoid(gk @ gk_up.astype(F32) + gk_b.astype(F32)) / G_TAU

    def heads(a, d):
        return a.reshape(B, T, G_HEADS, d).transpose(0, 2, 1, 3)

    o, s = gla_chunked(heads(q, G_DK) * (G_DK ** -0.5), heads(k, G_DK), heads(v, G_DV),
                       heads(la, G_DK), s0.astype(F32))
    o = rmsnorm(o.transpose(0, 2, 1, 3), norm_w)
    y = o.reshape(B, T, GROUP) * jax.nn.silu(g)
    return y.astype(pg.dtype), s


def moba_group(pa, pos0, k_past, v_past, q_norm, k_norm):
    B, T, _ = pa.shape
    q, k, v = jnp.split(pa, 3, axis=-1)
    shp = (B, T, A_HEADS, A_HEAD)
    pos = pos0 + jnp.arange(T)
    q = partial_rope(rmsnorm(q.reshape(shp), q_norm), pos)
    k = partial_rope(rmsnorm(k.reshape(shp), k_norm), pos)
    v = v.reshape(shp)
    if k_past is None:
        k_all, v_all = k, v
    else:
        k_all = jnp.concatenate([k_past.astype(k.dtype), k], axis=1)
        v_all = jnp.concatenate([v_past.astype(v.dtype), v], axis=1)
    o = moba_attention(q.transpose(0, 2, 1, 3), k_all.transpose(0, 2, 1, 3), v_all.transpose(0, 2, 1, 3), pos0)
    y = o.transpose(0, 2, 1, 3).reshape(B, T, GROUP).astype(pa.dtype)
    return y, k, v


def decoder_layer(x, pos0, state, k_past, v_past, lw):
    x = x + 0.5 * swiglu(rmsnorm(x, lw['ffn1_norm']), lw['ffn1_w_in'], lw['ffn1_w_out'])
    h = rmsnorm(x, lw['mix_norm'])
    proj = h @ lw['w_in']
    pm, pr, pg, pa = jnp.split(proj, [P_M, P_M + P_R, P_M + P_R + P_G], axis=-1)
    c, n, m, s_r, shift, s_g = state
    y_m, c, n, m = mlstm_group(pm, c, n, m, lw['mlstm_gate_b'], lw['mlstm_norm_w'])
    y_r, s_r, shift = rwkv_group(pr, s_r, shift, lw['rwkv_mu'], lw['rwkv_w0'], lw['rwkv_w2'], lw['rwkv_a0'],
                                 lw['rwkv_a2'], lw['rwkv_g2'], lw['rwkv_k_k'], lw['rwkv_k_a'], lw['rwkv_r_k'],
                                 lw['rwkv_ln_w'], lw['rwkv_ln_b'])
    y_g, s_g = gla_group(pg, s_g, lw['gla_gk_up'], lw['gla_gk_b'], lw['gla_norm_w'])
    y_a, k_new, v_new = moba_group(pa, pos0, k_past, v_past, lw['moba_q_norm'], lw['moba_k_norm'])
    x = x + jnp.concatenate([y_m, y_r, y_g, y_a], axis=-1) @ lw['w_out']
    x = x + 0.5 * swiglu(rmsnorm(x, lw['ffn2_norm']), lw['ffn2_w_in'], lw['ffn2_w_out'])
    return x, (c, n, m, s_r, shift, s_g), k_new, v_new


def zero_state(b):
    return (jnp.zeros((b, M_HEADS, M_DK, M_DV), F32), jnp.zeros((b, M_HEADS, M_DK), F32),
            jnp.zeros((b, M_HEADS), F32), jnp.zeros((b, R_HEADS, R_HEAD, R_HEAD), F32),
            jnp.zeros((b, P_R), F32), jnp.zeros((b, G_HEADS, G_DK, G_DV), F32))


def setup_inputs(seed: int = 0) -> dict:
    key = jax.random.key(seed)
    ks = iter(jax.random.split(key, 48))

    def nrm(shape, scale=1.0):
        return scale * jax.random.normal(next(ks), shape, F32)

    n_pages = PAST_LEN // PAGE_SIZE
    n_used = DEC_BATCH * n_pages
    n_phys = n_used + max(1, n_used // 4)
    page_table = jax.random.permutation(next(ks), n_phys)[:n_used].reshape(DEC_BATCH, n_pages).astype(jnp.int32)
    ones = lambda shape: 1.0 + nrm(shape, 0.02)
    return {
        'x_prompt': nrm((BATCH, SEQ, D_MODEL)),
        'x_sample': nrm((DEC_BATCH, DEC_SEQ, D_MODEL)),
        'cache_k': nrm((DEPTH, n_phys, PAGE_SIZE, A_HEADS, A_HEAD)),
        'cache_v': nrm((DEPTH, n_phys, PAGE_SIZE, A_HEADS, A_HEAD)),
        'state_mlstm_c': nrm((DEPTH, DEC_BATCH, M_HEADS, M_DK, M_DV), 0.5),
        'state_mlstm_n': nrm((DEPTH, DEC_BATCH, M_HEADS, M_DK), 0.5),
        'state_mlstm_m': nrm((DEPTH, DEC_BATCH, M_HEADS)),
        'state_rwkv': nrm((DEPTH, DEC_BATCH, R_HEADS, R_HEAD, R_HEAD), 0.5),
        'state_rwkv_shift': nrm((DEPTH, DEC_BATCH, P_R)),
        'state_gla': nrm((DEPTH, DEC_BATCH, G_HEADS, G_DK, G_DV), 0.5),
        'page_table': page_table,
        'ffn1_norm': ones((DEPTH, D_MODEL)),
        'ffn1_w_in': nrm((DEPTH, D_MODEL, 2 * D_FF), D_MODEL ** -0.5),
        'ffn1_w_out': nrm((DEPTH, D_FF, D_MODEL), D_FF ** -0.5),
        'mix_norm': ones((DEPTH, D_MODEL)),
        'w_in': nrm((DEPTH, D_MODEL, P_TOTAL), D_MODEL ** -0.5),
        'w_out': nrm((DEPTH, D_MODEL, D_MODEL), D_MODEL ** -0.5),
        'ffn2_norm': ones((DEPTH, D_MODEL)),
        'ffn2_w_in': nrm((DEPTH, D_MODEL, 2 * D_FF), D_MODEL ** -0.5),
        'ffn2_w_out': nrm((DEPTH, D_FF, D_MODEL), D_FF ** -0.5),
        'mlstm_gate_b': jnp.concatenate([nrm((DEPTH, M_HEADS), 0.1),
                                         jnp.linspace(3.0, 6.0, M_HEADS)[None] + nrm((DEPTH, M_HEADS), 0.1)], axis=-1),
        'mlstm_norm_w': ones((DEPTH, GROUP)),
        'rwkv_mu': jax.random.uniform(next(ks), (DEPTH, P_R), F32, 0.0, 1.0),
        'rwkv_w0': jnp.linspace(-6.0, -1.0, GROUP)[None] + nrm((DEPTH, GROUP), 0.1),
        'rwkv_w2': nrm((DEPTH, R_LORA_W, GROUP), 0.1),
        'rwkv_a0': nrm((DEPTH, GROUP), 0.1),
        'rwkv_a2': nrm((DEPTH, R_LORA_A, GROUP), 0.1),
        'rwkv_g2': nrm((DEPTH, R_LORA_G, GROUP), R_LORA_G ** -0.5),
        'rwkv_k_k': 0.85 + nrm((DEPTH, GROUP), 0.02),
        'rwkv_k_a': ones((DEPTH, GROUP)),
        'rwkv_r_k': nrm((DEPTH, R_HEADS, R_HEAD), 0.1),
        'rwkv_ln_w': ones((DEPTH, GROUP)),
        'rwkv_ln_b': nrm((DEPTH, GROUP), 0.02),
        'gla_gk_up': nrm((DEPTH, G_LORA, G_HEADS * G_DK), G_LORA ** -0.5),
        'gla_gk_b': nrm((DEPTH, G_HEADS * G_DK), 0.1),
        'gla_norm_w': ones((DEPTH, G_DV)),
        'moba_q_norm': ones((DEPTH, A_HEAD)),
        'moba_k_norm': ones((DEPTH, A_HEAD)),
    }


def reference(x_prompt, x_sample, cache_k, cache_v, state_mlstm_c, state_mlstm_n, state_mlstm_m, state_rwkv,
              state_rwkv_shift, state_gla, page_table, ffn1_norm, ffn1_w_in, ffn1_w_out, mix_norm, w_in, w_out,
              ffn2_norm, ffn2_w_in, ffn2_w_out, mlstm_gate_b, mlstm_norm_w, rwkv_mu, rwkv_w0, rwkv_w2, rwkv_a0,
              rwkv_a2, rwkv_g2, rwkv_k_k, rwkv_k_a, rwkv_r_k, rwkv_ln_w, rwkv_ln_b, gla_gk_up, gla_gk_b,
              gla_norm_w, moba_q_norm, moba_k_norm):
    db = x_sample.shape[0]
    past_len = page_table.shape[1] * cache_k.shape[2]
    y_p, y_s = x_prompt, x_sample
    new_p, new_s = [], []
    for l in range(DEPTH):
        lw = dict(ffn1_norm=ffn1_norm[l], ffn1_w_in=ffn1_w_in[l], ffn1_w_out=ffn1_w_out[l], mix_norm=mix_norm[l],
                  w_in=w_in[l], w_out=w_out[l], ffn2_norm=ffn2_norm[l], ffn2_w_in=ffn2_w_in[l],
                  ffn2_w_out=ffn2_w_out[l], mlstm_gate_b=mlstm_gate_b[l], mlstm_norm_w=mlstm_norm_w[l],
                  rwkv_mu=rwkv_mu[l], rwkv_w0=rwkv_w0[l], rwkv_w2=rwkv_w2[l], rwkv_a0=rwkv_a0[l],
                  rwkv_a2=rwkv_a2[l], rwkv_g2=rwkv_g2[l], rwkv_k_k=rwkv_k_k[l], rwkv_k_a=rwkv_k_a[l],
                  rwkv_r_k=rwkv_r_k[l], rwkv_ln_w=rwkv_ln_w[l], rwkv_ln_b=rwkv_ln_b[l], gla_gk_up=gla_gk_up[l],
                  gla_gk_b=gla_gk_b[l], gla_norm_w=gla_norm_w[l], moba_q_norm=moba_q_norm[l],
                  moba_k_norm=moba_k_norm[l])
        y_p, st_p, kp, vp = decoder_layer(y_p, 0, zero_state(x_prompt.shape[0]), None, None, lw)
        st_s = (state_mlstm_c[l], state_mlstm_n[l], state_mlstm_m[l], state_rwkv[l], state_rwkv_shift[l], state_gla[l])
        k_past = cache_k[l][page_table].reshape(db, past_len, A_HEADS, A_HEAD)
        v_past = cache_v[l][page_table].reshape(db, past_len, A_HEADS, A_HEAD)
        y_s, st_s, ks_, vs_ = decoder_layer(y_s, past_len, st_s, k_past, v_past, lw)
        new_p.append((kp, vp) + tuple(st_p))
        new_s.append((ks_, vs_) + tuple(st_s))

    def stack(items, i):
        return jnp.stack([it[i] for it in items])

    k_prompt, v_prompt = stack(new_p, 0), stack(new_p, 1)
    k_sample, v_sample = stack(new_s, 0), stack(new_s, 1)
    mlstm_c_prompt, mlstm_c_sample = stack(new_p, 2), stack(new_s, 2)
    mlstm_n_prompt, mlstm_n_sample = stack(new_p, 3), stack(new_s, 3)
    mlstm_m_prompt, mlstm_m_sample = stack(new_p, 4), stack(new_s, 4)
    rwkv_prompt, rwkv_sample = stack(new_p, 5), stack(new_s, 5)
    rwkv_shift_prompt, rwkv_shift_sample = stack(new_p, 6), stack(new_s, 6)
    gla_prompt, gla_sample = stack(new_p, 7), stack(new_s, 7)
    return (y_p, y_s, k_prompt, v_prompt, k_sample, v_sample, mlstm_c_prompt, mlstm_c_sample,
            mlstm_n_prompt, mlstm_n_sample, mlstm_m_prompt, mlstm_m_sample, rwkv_prompt, rwkv_sample,
            rwkv_shift_prompt, rwkv_shift_sample, gla_prompt, gla_sample)
```

```python
import functools

import jax
import jax.numpy as jnp
import numpy as np
from jax import lax
from jax.experimental import pallas as pl
from jax.experimental.pallas import tpu as pltpu

F32 = jnp.float32
BF16 = jnp.bfloat16
HI = lax.Precision.HIGHEST

D_MODEL = 2048
GROUP = D_MODEL // 4
M_HEADS = 4
M_DV = GROUP // M_HEADS
M_DK = M_DV // 2
GATE_CAP = 15.0
R_HEAD = 64
R_HEADS = GROUP // R_HEAD
R_LORA = 32
R_GN_EPS = 64e-5
G_HEADS = 4
G_DV = GROUP // G_HEADS
G_DK = G_DV // 2
G_LORA = 16
G_TAU = 16.0
A_HEAD = 64
A_HEADS = GROUP // A_HEAD
A_ROT = A_HEAD // 4
ROPE_THETA = 500000.0
MOBA_BLOCK = 256
MOBA_TOPK = 3
RMS_EPS = 1e-6
P_M = 2 * M_HEADS * M_DK + 2 * GROUP + 2 * M_HEADS
P_R = 3 * GROUP + 3 * R_LORA
P_G = 2 * G_HEADS * G_DK + 2 * GROUP + G_LORA
P_A = 3 * GROUP

CH = 64
GW = 1664
NPK = 4 * GW
NEG = -1e30
LANES = 128
VMEM_LIMIT = 48 * 1024 * 1024


def _dot(a, b, prec=None):
    return jnp.dot(a, b, preferred_element_type=F32, precision=prec)


def _dot_nt(a, b, prec=None):
    return lax.dot_general(a, b, (((1,), (1,)), ((), ())), preferred_element_type=F32, precision=prec)


def _dot_tn(a, b, prec=None):
    return lax.dot_general(a, b, (((0,), (0,)), ((), ())), preferred_element_type=F32, precision=prec)


def _iota(shape, dim):
    return lax.broadcasted_iota(jnp.int32, shape, dim)


def _sigmoid(x):
    return 1.0 / (1.0 + jnp.exp(-x))


def _softplus(x):
    return jnp.maximum(x, 0.0) + jnp.log(1.0 + jnp.exp(-jnp.abs(x)))


def _log_sigmoid(x):
    return -_softplus(-x)


def _tril(n):
    return (_iota((n, n), 1) <= _iota((n, n), 0)).astype(F32)


def _cparams(sem, vmem=VMEM_LIMIT):
    return pltpu.CompilerParams(dimension_semantics=sem, vmem_limit_bytes=vmem)


def _rms_rows(x, w):
    return x * lax.rsqrt(jnp.mean(x * x, axis=-1, keepdims=True) + RMS_EPS) * w


def _ffn_kernel(x_ref, nw_ref, wg_ref, wu_ref, wo_ref, o_ref, h_ref, acc_ref):
    j = pl.program_id(1)

    @pl.when(j == 0)
    def _():
        h_ref[...] = _rms_rows(x_ref[...], nw_ref[...]).astype(BF16)
        acc_ref[...] = jnp.zeros_like(acc_ref)

    h = h_ref[...]
    gate = _dot(h, wg_ref[...])
    up = _dot(h, wu_ref[...])
    act = (gate * _sigmoid(gate) * up).astype(BF16)
    acc_ref[...] += _dot(act, wo_ref[...])

    @pl.when(j == pl.num_programs(1) - 1)
    def _():
        o_ref[...] = x_ref[...] + 0.5 * acc_ref[...]


def _ffn(x, norm_w, w_in, w_out, *, tm, tf=512):
    rows, d = x.shape
    dff = w_out.shape[0]
    nj = dff // tf
    return pl.pallas_call(
        _ffn_kernel,
        grid=(rows // tm, nj),
        in_specs=[
            pl.BlockSpec((tm, d), lambda i, j: (i, 0)),
            pl.BlockSpec((1, d), lambda i, j: (0, 0)),
            pl.BlockSpec((d, tf), lambda i, j: (0, j)),
            pl.BlockSpec((d, tf), lambda i, j: (0, j + nj)),
            pl.BlockSpec((tf, d), lambda i, j: (j, 0)),
        ],
        out_specs=pl.BlockSpec((tm, d), lambda i, j: (i, 0)),
        out_shape=jax.ShapeDtypeStruct((rows, d), F32),
        scratch_shapes=[pltpu.VMEM((tm, d), BF16), pltpu.VMEM((tm, d), F32)],
        compiler_params=_cparams(("parallel", "arbitrary")),
        name="ffn",
    )(x, norm_w.reshape(1, d), w_in, w_in, w_out)


def _proj_kernel(x_ref, nw_ref, w_ref, o_ref, h_ref):
    @pl.when(pl.program_id(1) == 0)
    def _():
        h_ref[...] = _rms_rows(x_ref[...], nw_ref[...]).astype(BF16)

    o_ref[...] = _dot(h_ref[...], w_ref[...])


def _proj(x, norm_w, w, *, tm, tn=512):
    rows, d = x.shape
    n = w.shape[1]
    return pl.pallas_call(
        _proj_kernel,
        grid=(rows // tm, n // tn),
        in_specs=[
            pl.BlockSpec((tm, d), lambda i, j: (i, 0)),
            pl.BlockSpec((1, d), lambda i, j: (0, 0)),
            pl.BlockSpec((d, tn), lambda i, j: (0, j)),
        ],
        out_specs=pl.BlockSpec((tm, tn), lambda i, j: (i, j)),
        out_shape=jax.ShapeDtypeStruct((rows, n), F32),
        scratch_shapes=[pltpu.VMEM((tm, d), BF16)],
        compiler_params=_cparams(("parallel", "arbitrary")),
        name="proj",
    )(x, norm_w.reshape(1, d), w)


def _outproj_kernel(x_ref, ym_ref, yr_ref, yg_ref, ya_ref, w_ref, o_ref):
    acc = x_ref[...]
    for gi, y_ref in enumerate((ym_ref, yr_ref, yg_ref, ya_ref)):
        acc = acc + _dot(y_ref[...].astype(BF16), w_ref[gi * GROUP:(gi + 1) * GROUP, :])
    o_ref[...] = acc


def _outproj(x, ym, yr, yg, ya, w, *, tm):
    rows, d = x.shape
    yspec = pl.BlockSpec((tm, GROUP), lambda i: (i, 0))
    return pl.pallas_call(
        _outproj_kernel,
        grid=(rows // tm,),
        in_specs=[pl.BlockSpec((tm, d), lambda i: (i, 0)), yspec, yspec, yspec, yspec,
                  pl.BlockSpec((d, d), lambda i: (0, 0))],
        out_specs=pl.BlockSpec((tm, d), lambda i: (i, 0)),
        out_shape=jax.ShapeDtypeStruct((rows, d), F32),
        compiler_params=_cparams(("parallel",)),
        name="outproj",
    )(x, ym, yr, yg, ya, w)


def _const_spec(shape):
    return pl.BlockSpec(shape, lambda g: (0,) * len(shape))


def _mixer_specs(npc, group, state_blocks):
    def chunk_idx(s):
        return lambda g: (jnp.where(g < npc, s * npc + g, 2 * npc + 2 * (g - npc) + s), 0, group)

    p_specs = [pl.BlockSpec((None, CH, GW), chunk_idx(s)) for s in (0, 1)]
    in_state, out_state = [], []
    for blk in state_blocks:
        zeros = (0,) * len(blk)
        for s in (0, 1):
            in_state.append(pl.BlockSpec(
                (None,) + blk, lambda g, s=s, z=zeros: (jnp.where(g < npc, s, 2 + 2 * (g - npc) + s),) + z))
            out_state.append(pl.BlockSpec(
                (None,) + blk, lambda g, z=zeros: (jnp.where(g < npc, 0, 1 + g - npc),) + z))
    y_specs = [pl.BlockSpec((2, None, CH, GROUP), lambda g: (0, jnp.minimum(g, npc - 1), 0, 0)),
               pl.BlockSpec((None, 2, CH, GROUP), lambda g: (jnp.maximum(g - npc, 0), 0, 0, 0))]
    return p_specs, in_state, out_state, y_specs


def _mixer_out_shapes(npc, n_sample, state_blocks):
    ys = [jax.ShapeDtypeStruct((2, npc, CH, GROUP), F32), jax.ShapeDtypeStruct((n_sample // 2, 2, CH, GROUP), F32)]
    st = []
    for blk in state_blocks:
        st += [jax.ShapeDtypeStruct((1 + n_sample // 2,) + blk, F32)] * 2
    return ys, st


def _step_info(npc, t_s):
    g = pl.program_id(0)
    first = jnp.logical_or(g == 0, g >= npc)
    last = g >= npc - 1
    tlen = jnp.where(g < npc, CH, t_s)
    return g, first, last, tlen


def _write_y(g, npc, yp_ref, ys_ref, s, val):
    @pl.when(g < npc)
    def _():
        yp_ref[s] = val

    @pl.when(g >= npc)
    def _():
        ys_ref[s] = val


def _merge_y(yp, ys):
    return jnp.concatenate([yp.reshape(-1, GROUP), ys.reshape(-1, GROUP)], axis=0)


def _merge_state(a, b):
    prompt = jnp.stack([a[0], b[0]])
    sample = jnp.stack([a[1:], b[1:]], axis=1).reshape((-1,) + a.shape[1:])
    return prompt, sample


def _mlstm_kernel(pa_ref, pb_ref, cea_ref, ceb_ref, ma_ref, mb_ref, gb_ref, nw_ref,
                  yp_ref, ys_ref, ceoa_ref, ceob_ref, moa_ref, mob_ref, ce_sc, m_sc, *, npc, t_s):
    g, first, last, tlen = _step_info(npc, t_s)
    slots = ((pa_ref, cea_ref, ma_ref, ceoa_ref, moa_ref), (pb_ref, ceb_ref, mb_ref, ceob_ref, mob_ref))
    row = _iota((CH, LANES), 0)
    lane = _iota((CH, LANES), 1)
    tril = _tril(CH)
    causal = _iota((CH, CH), 1) <= _iota((CH, CH), 0)
    e0 = (lane == 0).astype(F32)
    for s, (p_ref, ce0_ref, m0_ref, ceo_ref, mo_ref) in enumerate(slots):
        @pl.when(first)
        def _():
            ce_sc[s] = ce0_ref[...]
            m_sc[s] = m0_ref[...]

        valid = row < tlen
        gates = p_ref[:, 1536:1664] + gb_ref[...]
        gates = GATE_CAP * jnp.tanh(gates / GATE_CAP)
        ig = jnp.where(valid, gates, NEG)
        lf = jnp.where(valid, _log_sigmoid(gates), 0.0)
        b_col = _dot(tril, lf, HI)
        ig_t = ig.T
        b_row = _dot_nt(lf.T[0:8], tril, HI)
        ys = []
        for h in range(M_HEADS):
            pair, half = h // 2, h % 2
            hm = (lane // 64) == half
            qm = jnp.where(hm, p_ref[:, 128 * pair:128 * pair + 128], 0.0)
            km = jnp.where(hm, p_ref[:, 256 + 128 * pair:256 + 128 * pair + 128], 0.0) * (M_DK ** -0.5)
            v_ext = jnp.concatenate([p_ref[:, 512 + 128 * h:512 + 128 * h + 128], e0], axis=1)
            ce = ce_sc[s, h]
            m_h = m_sc[s, 0:1, h:h + 1]
            b_c = b_col[:, 4 + h:5 + h]
            i_c = ig[:, h:h + 1]
            b_r = b_row[4 + h:5 + h, :]
            i_r = ig_t[h:h + 1, :]
            dmat = jnp.where(causal, b_c - b_r + i_r, NEG)
            m_inter = b_c + m_h
            mt = jnp.maximum(m_inter, jnp.max(dmat, axis=1, keepdims=True))
            s_inter = jnp.exp(m_inter - mt)
            w = jnp.exp(dmat - mt) * _dot_nt(qm, km, HI)
            nd = s_inter * _dot(qm, ce, HI) + _dot(w, v_ext, HI)
            num = nd[:, 0:128]
            den = nd[:, 128:129]
            hh = num / jnp.maximum(jnp.abs(den), jnp.exp(-mt))
            m_new = mt[CH - 1:CH, :]
            b_last = b_c[CH - 1:CH, :]
            carry = jnp.exp(b_last + m_h - m_new)
            ws = jnp.exp(b_last - b_c + i_c - m_new)
            ce_sc[s, h] = carry * ce + _dot_tn(km, ws * v_ext, HI)
            m_sc[s, 0:1, h:h + 1] = m_new
            hn = hh * lax.rsqrt(jnp.mean(hh * hh, axis=-1, keepdims=True) + RMS_EPS)
            hn = hn * nw_ref[:, 128 * h:128 * h + 128]
            ys.append(hn * _sigmoid(p_ref[:, 1024 + 128 * h:1024 + 128 * h + 128]))
        _write_y(g, npc, yp_ref, ys_ref, s, jnp.concatenate(ys, axis=1))

        @pl.when(last)
        def _():
            ceo_ref[...] = ce_sc[s]
            mo_ref[...] = m_sc[s]


def _mlstm(p3, ce0, m0, gate_b, norm_w, *, npc, n_sample, t_s):
    blocks = [(M_HEADS, 128, 256), (8, LANES)]
    p_specs, in_state, out_state, y_specs = _mixer_specs(npc, 0, blocks)
    y_shapes, st_shapes = _mixer_out_shapes(npc, n_sample, blocks)
    gb = jnp.zeros((1, LANES), F32).at[0, :2 * M_HEADS].set(gate_b)
    return pl.pallas_call(
        functools.partial(_mlstm_kernel, npc=npc, t_s=t_s),
        grid=(npc + n_sample // 2,),
        in_specs=p_specs + in_state + [_const_spec((1, LANES)), _const_spec((1, GROUP))],
        out_specs=y_specs + out_state,
        out_shape=y_shapes + st_shapes,
        scratch_shapes=[pltpu.VMEM((2, M_HEADS, 128, 256), F32), pltpu.VMEM((2, 8, LANES), F32)],
        compiler_params=_cparams(("arbitrary",)),
        name="mlstm",
    )(p3, p3, ce0, ce0, m0, m0, gb, norm_w.reshape(1, GROUP))


def _gla_kernel(pa_ref, pb_ref, sa_ref, sb_ref, up_ref, gkb_ref, nw_ref,
                yp_ref, ys_ref, soa_ref, sob_ref, s_sc, k_sc, b_sc, *, npc, t_s):
    g, first, last, tlen = _step_info(npc, t_s)
    slots = ((pa_ref, sa_ref, soa_ref), (pb_ref, sb_ref, sob_ref))
    row = _iota((CH, LANES), 0)
    lane = _iota((CH, LANES), 1)
    tril = _tril(CH)
    r128 = _iota((LANES, LANES), 0)
    c128 = _iota((LANES, LANES), 1)
    same_half = (r128 // 64) == (c128 // 64)
    eye = (r128 == c128).astype(F32)
    for s, (p_ref, s0_ref, so_ref) in enumerate(slots):
        @pl.when(first)
        def _():
            s_sc[s] = s0_ref[...]

        valid = row < tlen
        z = _dot(p_ref[:, 1536:1664], up_ref[...], HI) + gkb_ref[...]
        la = _log_sigmoid(z) / G_TAU
        la = jnp.where(jnp.concatenate([valid, valid], axis=1), la, 0.0)
        b = _dot(tril, la, HI)
        ys = []
        for pair in range(2):
            sl = slice(128 * pair, 128 * pair + 128)
            qp = p_ref[:, sl] * (G_DK ** -0.5)
            kp = jnp.where(valid, p_ref[:, 256 + 128 * pair:256 + 128 * pair + 128], 0.0)
            bp = b[:, sl]
            bl = bp[CH - 1:CH, :]

            k_sc[s, pair] = kp
            b_sc[s, pair] = bp

            def body(ti, acc, s=s, pair=pair, qp=qp, bp=bp):
                r0 = pl.multiple_of(ti * 8, 8)
                k8 = k_sc[s, pair, pl.ds(r0, 8), :]
                b8 = b_sc[s, pair, pl.ds(r0, 8), :]
                for j in range(8):
                    e = qp * k8[j:j + 1, :] * jnp.exp(jnp.minimum(bp - b8[j:j + 1, :], 0.0))
                    place = jnp.where(jnp.logical_and(same_half, (c128 % 64) == r0 + j), 1.0, 0.0)
                    acc = acc + _dot(e, place, HI)
                return acc

            att = lax.fori_loop(0, CH // 8, body, jnp.zeros((CH, LANES), F32))
            att = jnp.where((lane % 64) <= row, att, 0.0)
            for half in range(2):
                h = 2 * pair + half
                hm = (lane // 64) == half
                st = s_sc[s, h]
                v_h = p_ref[:, 512 + 128 * h:512 + 128 * h + 128]
                qe = jnp.where(hm, qp * jnp.exp(bp), 0.0)
                o = _dot(qe, st, HI) + _dot(jnp.where(hm, att, 0.0), jnp.concatenate([v_h, v_h], axis=0), HI)
                khat = jnp.where(hm, kp * jnp.exp(bl - bp), 0.0)
                dg = eye * jnp.exp(bl)
                s_sc[s, h] = _dot_tn(jnp.concatenate([khat, dg], axis=0), jnp.concatenate([v_h, st], axis=0), HI)
                on = o * lax.rsqrt(jnp.mean(o * o, axis=-1, keepdims=True) + RMS_EPS) * nw_ref[...]
                gt = p_ref[:, 1024 + 128 * h:1024 + 128 * h + 128]
                ys.append(on * gt * _sigmoid(gt))
        _write_y(g, npc, yp_ref, ys_ref, s, jnp.concatenate(ys, axis=1))

        @pl.when(last)
        def _():
            so_ref[...] = s_sc[s]


def _gla(p3, s0, gk_up, gk_b, norm_w, *, npc, n_sample, t_s):
    blocks = [(G_HEADS, 128, 128)]
    p_specs, in_state, out_state, y_specs = _mixer_specs(npc, 2, blocks)
    y_shapes, st_shapes = _mixer_out_shapes(npc, n_sample, blocks)
    up = jnp.zeros((LANES, G_HEADS * G_DK), F32).at[:G_LORA].set(gk_up)
    return pl.pallas_call(
        functools.partial(_gla_kernel, npc=npc, t_s=t_s),
        grid=(npc + n_sample // 2,),
        in_specs=p_specs + in_state + [_const_spec((LANES, 256)), _const_spec((1, 256)), _const_spec((1, G_DV))],
        out_specs=y_specs + out_state,
        out_shape=y_shapes + st_shapes,
        scratch_shapes=[pltpu.VMEM((2, G_HEADS, 128, 128), F32), pltpu.VMEM((2, 2, CH, LANES), F32),
                        pltpu.VMEM((2, 2, CH, LANES), F32)],
        compiler_params=_cparams(("arbitrary",)),
        name="gla",
    )(p3, p3, s0, s0, up, gk_b.reshape(1, -1), norm_w.reshape(1, G_DV))


def _rwkv_kernel(pa_ref, pb_ref, wa_ref, wb_ref, sha_ref, shb_ref, mu_ref, w0_ref, w2_ref, a0_ref, a2_ref,
                 g2_ref, kk_ref, ka_ref, rk_ref, lnw_ref, lnb_ref, bones_ref,
                 yp_ref, ys_ref, woa_ref, wob_ref, shoa_ref, shob_ref, w_sc, sh_sc, *, npc, t_s):
    g, first, last, tlen = _step_info(npc, t_s)
    slots = ((pa_ref, wa_ref, sha_ref, woa_ref, shoa_ref), (pb_ref, wb_ref, shb_ref, wob_ref, shob_ref))
    lane = _iota((CH, LANES), 1)
    tril = _tril(CH)
    r64, c64 = _iota((CH, CH), 0), _iota((CH, CH), 1)
    lower_strict = c64 < r64
    lower = c64 <= r64
    eye64 = (r64 == c64).astype(F32)
    eye128 = (_iota((LANES, LANES), 0) == _iota((LANES, LANES), 1)).astype(F32)
    bones = bones_ref[...]
    for s, (p_ref, w0s_ref, sh0_ref, wo_ref, sho_ref) in enumerate(slots):
        @pl.when(first)
        def _():
            w_sc[s] = w0s_ref[...]
            sh_sc[s] = sh0_ref[...]

        pf = p_ref[...]
        prev = jnp.where(_iota((CH, GW), 0) == 0, sh_sc[s, 0:1, :], pltpu.roll(pf, 1, 0))
        xs = pf + (prev - pf) * mu_ref[...]
        valid = _iota((CH, GROUP), 0) < tlen
        r = xs[:, 0:512]
        k = xs[:, 512:1024]
        v = xs[:, 1024:1536]
        lo = xs[:, 1536:1664]
        wraw = -_softplus(-(w0_ref[...] + _dot(jnp.tanh(lo), w2_ref[...], HI))) - 0.5
        logw = jnp.where(valid, -jnp.exp(wraw), 0.0)
        a = _sigmoid(a0_ref[...] + _dot(lo, a2_ref[...], HI))
        gg = _dot(_sigmoid(lo), g2_ref[...], HI)
        kk = k * kk_ref[...]
        kk = kk / jnp.maximum(jnp.sqrt(_dot(kk * kk, bones, HI)), 1e-12)
        k2 = k * (1.0 + (a - 1.0) * ka_ref[...])
        k2m = jnp.where(valid, k2, 0.0)
        bv = jnp.where(valid, kk * a, 0.0)
        lg = _dot(tril, logw, HI)
        lgl = lg[CH - 1:CH, :]
        e_out = jnp.exp(-lg)
        e_end = jnp.exp(lgl - lg)
        at = -kk * jnp.exp(lg - logw)
        rt = r * jnp.exp(lg)
        bt = bv * e_out
        kt = k2m * e_out
        bh = bv * e_end
        kh = k2m * e_end
        gl = jnp.exp(lgl)
        ys = []
        for pair in range(R_HEADS // 2):
            sl = slice(128 * pair, 128 * pair + 128)
            vp = v[:, sl]
            yh = []
            for half in range(2):
                h = 2 * pair + half
                hm = (lane // 64) == half
                ar = jnp.concatenate([jnp.where(hm, at[:, sl], 0.0), jnp.where(hm, rt[:, sl], 0.0)], axis=0)
                gb = _dot_nt(ar, bt[:, sl], HI)
                gk = _dot_nt(ar, kt[:, sl], HI)
                n_ab = jnp.where(lower_strict, gb[0:CH], 0.0)
                n_ak = jnp.where(lower_strict, gk[0:CH], 0.0)
                rb = jnp.where(lower, gb[CH:2 * CH], 0.0)
                rkm = jnp.where(lower, gk[CH:2 * CH], 0.0)
                tm = eye64 + n_ab
                mx = n_ab
                for _ in range(5):
                    mx = _dot(mx, mx, HI)
                    tm = tm + _dot(tm, mx, HI)
                wst = w_sc[s, h]
                arw = _dot(ar, wst, HI)
                pm = _dot(tm, arw[0:CH] + _dot(n_ak, vp, HI), HI)
                yh.append(arw[CH:2 * CH] + _dot(rb, pm, HI) + _dot(rkm, vp, HI))
                lhs = jnp.concatenate([jnp.where(hm, bh[:, sl], 0.0), jnp.where(hm, kh[:, sl], 0.0),
                                       eye128 * gl[:, sl]], axis=0)
                w_sc[s, h] = _dot_tn(lhs, jnp.concatenate([pm, vp, wst], axis=0), HI)
            ys.append(jnp.where(lane < 64, yh[0], yh[1]))
        y = jnp.concatenate(ys, axis=1)
        mean = _dot(y, bones, HI) * (1.0 / R_HEAD)
        d = y - mean
        var = _dot(d * d, bones, HI) * (1.0 / R_HEAD)
        yn = d * lax.rsqrt(var + R_GN_EPS) * lnw_ref[...] + lnb_ref[...]
        bonus = _dot(r * k2 * rk_ref[...], bones, HI) * v
        _write_y(g, npc, yp_ref, ys_ref, s, (yn + bonus) * gg)
        last_row = jnp.where(g < npc, p_ref[CH - 1:CH, :], p_ref[t_s - 1:t_s, :])
        sh_sc[s] = jnp.broadcast_to(last_row, (8, GW))

        @pl.when(last)
        def _():
            wo_ref[...] = w_sc[s]
            sho_ref[...] = sh_sc[s]


def _head_block_ones(width, head):
    hid = np.arange(width) // head
    return jnp.asarray((hid[:, None] == hid[None, :]).astype(np.float32))


def _rwkv(p3, w0s, sh0, lw, *, npc, n_sample, t_s):
    blocks = [(R_HEADS, 128, 128), (8, GW)]
    p_specs, in_state, out_state, y_specs = _mixer_specs(npc, 1, blocks)
    y_shapes, st_shapes = _mixer_out_shapes(npc, n_sample, blocks)

    def lora(w, off):
        return jnp.zeros((LANES, GROUP), F32).at[off:off + R_LORA].set(w)

    def row(t):
        return t.reshape(1, GROUP)

    mu = jnp.zeros((1, GW), F32).at[0, :P_R].set(lw['rwkv_mu'])
    consts = [mu, row(lw['rwkv_w0']), lora(lw['rwkv_w2'], 0), row(lw['rwkv_a0']), lora(lw['rwkv_a2'], R_LORA),
              lora(lw['rwkv_g2'], 2 * R_LORA), row(lw['rwkv_k_k']), row(lw['rwkv_k_a']), row(lw['rwkv_r_k']),
              row(lw['rwkv_ln_w']), row(lw['rwkv_ln_b']), _head_block_ones(GROUP, R_HEAD)]
    return pl.pallas_call(
        functools.partial(_rwkv_kernel, npc=npc, t_s=t_s),
        grid=(npc + n_sample // 2,),
        in_specs=p_specs + in_state + [_const_spec(c.shape) for c in consts],
        out_specs=y_specs + out_state,
        out_shape=y_shapes + st_shapes,
        scratch_shapes=[pltpu.VMEM((2, R_HEADS, 128, 128), F32), pltpu.VMEM((2, 8, GW), F32)],
        compiler_params=_cparams(("arbitrary",)),
        name="rwkv",
    )(p3, p3, w0s, w0s, sh0, sh0, *consts)


def _moba_prep_kernel(p_ref, cos_ref, sin_ref, qn_ref, kn_ref, bones_ref, q_ref, k_ref, kb_ref, vb_ref, km_ref):
    bones = bones_ref[...]
    cos = jnp.concatenate([cos_ref[...]] * 4, axis=1)
    sin = jnp.concatenate([sin_ref[...]] * 4, axis=1)
    lane = _iota(cos.shape, 1)
    low = (lane % A_HEAD) < (A_ROT // 2)

    def norm_rope(x, w):
        ms = _dot(x * x, bones, HI) * (1.0 / A_HEAD)
        xn = x * lax.rsqrt(ms + RMS_EPS) * w
        partner = jnp.where(low, pltpu.roll(xn, GROUP - A_ROT // 2, 1), pltpu.roll(xn, A_ROT // 2, 1))
        return xn * cos + partner * sin

    q_ref[...] = norm_rope(p_ref[:, 0:512], qn_ref[...])
    kr = norm_rope(p_ref[:, 512:1024], kn_ref[...])
    k_ref[...] = kr
    kb_ref[...] = kr.astype(BF16)
    vb_ref[...] = p_ref[:, 1024:1536].astype(BF16)
    km_ref[...] = jnp.mean(kr, axis=0, keepdims=True)


def _moba_prep(p, cos_t, sin_t, q_norm, k_norm):
    rows = p.shape[0]
    nt = rows // MOBA_BLOCK
    rspec = pl.BlockSpec((MOBA_BLOCK, GROUP), lambda i: (i, 0))
    tspec = pl.BlockSpec((MOBA_BLOCK, LANES), lambda i: (i, 0))
    cspec = pl.BlockSpec((1, GROUP), lambda i: (0, 0))
    return pl.pallas_call(
        _moba_prep_kernel,
        grid=(nt,),
        in_specs=[pl.BlockSpec((MOBA_BLOCK, GW), lambda i: (i, 3)), tspec, tspec, cspec, cspec,
                  pl.BlockSpec((GROUP, GROUP), lambda i: (0, 0))],
        out_specs=[rspec, rspec, rspec, rspec, pl.BlockSpec((None, 1, GROUP), lambda i: (i, 0, 0))],
        out_shape=[jax.ShapeDtypeStruct((rows, GROUP), F32), jax.ShapeDtypeStruct((rows, GROUP), F32),
                   jax.ShapeDtypeStruct((rows, GROUP), BF16), jax.ShapeDtypeStruct((rows, GROUP), BF16),
                   jax.ShapeDtypeStruct((nt, 1, GROUP), F32)],
        compiler_params=_cparams(("parallel",)),
        name="moba_prep",
    )(p, cos_t, sin_t, jnp.tile(q_norm, A_HEADS).reshape(1, GROUP), jnp.tile(k_norm, A_HEADS).reshape(1, GROUP),
      _head_block_ones(GROUP, A_HEAD))


def _moba_prompt_kernel(q_ref, kb_ref, vb_ref, km_ref, y_ref, *, nb):
    i = pl.program_id(1)
    blk = MOBA_BLOCK
    scale = A_HEAD ** -0.5
    lane = _iota((blk, LANES), 1)
    bidx = _iota((blk, nb), 1)
    causal = _iota((blk, blk), 1) <= _iota((blk, blk), 0)
    row0 = pl.multiple_of(i * blk, blk)
    for pair in range(A_HEADS // 2):
        sl = slice(128 * pair, 128 * pair + 128)
        qp = q_ref[:, sl]
        kmp = km_ref[:, sl]
        k_own = kb_ref[pl.ds(row0, blk), sl]
        v_own = vb_ref[pl.ds(row0, blk), sl]
        outs = []
        for half in range(2):
            hm = (lane // 64) == half
            qm = jnp.where(hm, qp, 0.0)
            qb = qm.astype(BF16)
            gate = jnp.where(bidx < i, _dot_nt(qm, kmp, HI), -jnp.inf)
            picks = []
            for _ in range(MOBA_TOPK):
                mx = jnp.max(gate, axis=1, keepdims=True)
                idx = jnp.min(jnp.where(gate == mx, bidx, nb), axis=1, keepdims=True)
                picks.append(jnp.where(mx > -jnp.inf, idx, -1))
                gate = jnp.where(bidx == idx, -jnp.inf, gate)
            s_own = jnp.where(causal, _dot_nt(qb, k_own) * scale, NEG)
            m0 = jnp.max(s_own, axis=1, keepdims=True)
            p0 = jnp.exp(s_own - m0)
            l0 = jnp.sum(p0, axis=1, keepdims=True)
            acc0 = _dot(p0.astype(BF16), v_own)

            def body(j, carry, pair=pair, qb=qb, picks=picks):
                m, l, acc = carry
                r0 = pl.multiple_of(j * blk, blk)
                kj = kb_ref[pl.ds(r0, blk), 128 * pair:128 * pair + 128]
                vj = vb_ref[pl.ds(r0, blk), 128 * pair:128 * pair + 128]
                sel = jnp.logical_or(jnp.logical_or(picks[0] == j, picks[1] == j), picks[2] == j)
                sj = jnp.where(sel, _dot_nt(qb, kj) * scale, NEG)
                m_new = jnp.maximum(m, jnp.max(sj, axis=1, keepdims=True))
                alpha = jnp.exp(m - m_new)
                pj = jnp.exp(sj - m_new)
                return (m_new, alpha * l + jnp.sum(pj, axis=1, keepdims=True),
                        alpha * acc + _dot(pj.astype(BF16), vj))

            m, l, acc = lax.fori_loop(0, i, body, (m0, l0, acc0))
            outs.append(acc / l)
        y_ref[:, sl] = jnp.where(lane < 64, outs[0], outs[1])


def _moba_prompt(q, kb, vb, km, *, nbatch, t):
    nb = t // MOBA_BLOCK
    return pl.pallas_call(
        functools.partial(_moba_prompt_kernel, nb=nb),
        grid=(nbatch, nb),
        in_specs=[pl.BlockSpec((MOBA_BLOCK, GROUP), lambda b, i: (b * nb + i, 0)),
                  pl.BlockSpec((t, GROUP), lambda b, i: (b, 0)),
                  pl.BlockSpec((t, GROUP), lambda b, i: (b, 0)),
                  pl.BlockSpec((None, nb, GROUP), lambda b, i: (b, 0, 0))],
        out_specs=pl.BlockSpec((MOBA_BLOCK, GROUP), lambda b, i: (b * nb + i, 0)),
        out_shape=jax.ShapeDtypeStruct((nbatch * t, GROUP), F32),
        compiler_params=_cparams(("parallel", "arbitrary")),
        name="moba_prompt",
    )(q, kb, vb, km)


def _moba_gate_kernel(pt_ref, *refs, npages_step, nblocks, page):
    del pt_ref
    pg_refs = refs[:npages_step]
    q_ref, sel_ref, km_sc = refs[npages_step:]
    st = pl.program_id(1)
    per = npages_step // 2
    inv = 1.0 / (2 * page)
    for i in range(per):
        tot = jnp.sum(pg_refs[2 * i][...], axis=0) + jnp.sum(pg_refs[2 * i + 1][...], axis=0)
        km_sc[st * per + i] = tot * inv

    @pl.when(st == pl.num_programs(1) - 1)
    def _():
        km = km_sc[...]
        nidx = _iota((nblocks, A_HEADS, 1), 0)
        for qi in range(q_ref.shape[0]):
            gate = jnp.sum(km * q_ref[qi][None], axis=-1, keepdims=True)
            for slot in range(MOBA_TOPK):
                mx = jnp.max(gate, axis=0, keepdims=True)
                idx = jnp.min(jnp.where(gate == mx, nidx, nblocks), axis=0, keepdims=True)
                sel_ref[qi * MOBA_TOPK + slot] = jnp.broadcast_to(idx[0], (A_HEADS, LANES))
                gate = jnp.where(nidx == idx, -jnp.inf, gate)


def _moba_gate(cache_k, layer, page_table, q_s):
    nseq, npages = page_table.shape
    page = cache_k.shape[2]
    nblocks = npages * page // MOBA_BLOCK
    npages_step = 8
    tq = q_s.shape[1]

    def pg_spec(i):
        return pl.BlockSpec((None, None, page, A_HEADS, A_HEAD),
                            lambda b, st, pt: (layer, pt[b, st * npages_step + i], 0, 0, 0))

    return pl.pallas_call(
        functools.partial(_moba_gate_kernel, npages_step=npages_step, nblocks=nblocks, page=page),
        grid_spec=pltpu.PrefetchScalarGridSpec(
            num_scalar_prefetch=1,
            grid=(nseq, npages // npages_step),
            in_specs=[pg_spec(i) for i in range(npages_step)]
            + [pl.BlockSpec((None, tq, A_HEADS, A_HEAD), lambda b, st, pt: (b, 0, 0, 0))],
            out_specs=pl.BlockSpec((None, tq * MOBA_TOPK, A_HEADS, LANES), lambda b, st, pt: (b, 0, 0, 0)),
            scratch_shapes=[pltpu.VMEM((nblocks, A_HEADS, A_HEAD), F32)]),
        out_shape=jax.ShapeDtypeStruct((nseq, tq * MOBA_TOPK, A_HEADS, LANES), jnp.int32),
        compiler_params=_cparams(("parallel", "arbitrary")),
        name="moba_gate",
    )(page_table, *([cache_k] * npages_step), q_s)


def _moba_sample_kernel(pg_ref, q_ref, kn_ref, vn_ref, ck_ref, cv_ref, o_ref, kbuf, vbuf, sem, *, layer, tq, page):
    b, h = pl.program_id(0), pl.program_id(1)
    nslab = tq * MOBA_TOPK * (MOBA_BLOCK // page)
    base = (b * pl.num_programs(1) + h) * nslab

    def copies(c):
        pg = pg_ref[base + c]
        return (pltpu.make_async_copy(ck_ref.at[layer, pg, :, h, :], kbuf.at[c], sem.at[0]),
                pltpu.make_async_copy(cv_ref.at[layer, pg, :, h, :], vbuf.at[c], sem.at[1]))

    for c in range(nslab):
        for cp in copies(c):
            cp.start()
    for c in range(nslab):
        for cp in copies(c):
            cp.wait()
    scale = A_HEAD ** -0.5
    nk = nslab * page
    q = q_ref[...]
    kall = kbuf[...].reshape(nk, A_HEAD)
    vall = vbuf[...].reshape(nk, A_HEAD)
    rows = q.shape[0]
    owner = _iota((rows, nk), 1) // (MOBA_TOPK * MOBA_BLOCK)
    s_sel = jnp.where(owner == _iota((rows, nk), 0), _dot_nt(q, kall) * scale, NEG)
    r8, c8 = _iota((rows, rows), 0), _iota((rows, rows), 1)
    s_own = jnp.where(jnp.logical_and(c8 <= r8, c8 < tq), _dot_nt(q, kn_ref[...]) * scale, NEG)
    m = jnp.maximum(jnp.max(s_sel, axis=1, keepdims=True), jnp.max(s_own, axis=1, keepdims=True))
    p_sel = jnp.exp(s_sel - m)
    p_own = jnp.exp(s_own - m)
    l = jnp.sum(p_sel, axis=1, keepdims=True) + jnp.sum(p_own, axis=1, keepdims=True)
    o_ref[...] = (_dot(p_sel, vall) + _dot(p_own, vn_ref[...])) / l


def _moba_sample(pages, q_s, kn_s, vn_s, cache_k, cache_v, layer, *, tq):
    nseq, nh, rows, hd = q_s.shape
    page = cache_k.shape[2]
    nslab = tq * MOBA_TOPK * (MOBA_BLOCK // page)
    spec = pl.BlockSpec((None, None, rows, hd), lambda b, h, pg: (b, h, 0, 0))
    return pl.pallas_call(
        functools.partial(_moba_sample_kernel, layer=layer, tq=tq, page=page),
        grid_spec=pltpu.PrefetchScalarGridSpec(
            num_scalar_prefetch=1,
            grid=(nseq, nh),
            in_specs=[spec, spec, spec, pl.BlockSpec(memory_space=pl.ANY), pl.BlockSpec(memory_space=pl.ANY)],
            out_specs=spec,
            scratch_shapes=[pltpu.VMEM((nslab, page, hd), F32), pltpu.VMEM((nslab, page, hd), F32),
                            pltpu.SemaphoreType.DMA((2,))]),
        out_shape=jax.ShapeDtypeStruct((nseq, nh, rows, hd), F32),
        compiler_params=_cparams(("arbitrary", "arbitrary")),
        name="moba_sample",
    )(pages, q_s, kn_s, vn_s, cache_k, cache_v)


def _pack_w_in(w):
    def dz(n):
        return jnp.zeros(w.shape[:2] + (n,), w.dtype)

    o_r, o_g, o_a = P_M, P_M + P_R, P_M + P_R + P_G
    kd = 2 * G_HEADS * G_DK + GROUP
    parts = [w[..., 0:P_M], dz(GW - P_M),
             w[..., o_r:o_r + P_R], dz(GW - P_R),
             w[..., o_g:o_g + kd], w[..., o_g + kd + G_LORA:o_g + P_G], w[..., o_g + kd:o_g + kd + G_LORA],
             dz(GW - P_G),
             w[..., o_a:o_a + P_A], dz(GW - P_A)]
    return jnp.concatenate(parts, axis=-1).astype(BF16)


def _even_head(nheads):
    return (jnp.arange(nheads) % 2 == 0).reshape((nheads, 1, 1))


def _place_half(x, nheads, axis):
    z = jnp.zeros_like(x)
    even = _even_head(nheads)
    return jnp.concatenate([jnp.where(even, x, z), jnp.where(even, z, x)], axis=axis)


def _take_half(x, nheads, axis):
    even = _even_head(nheads)
    lo = lax.slice_in_dim(x, 0, 64, axis=x.ndim + axis)
    hi = lax.slice_in_dim(x, 64, 128, axis=x.ndim + axis)
    return jnp.where(even, lo, hi)


def kernel(x_prompt, x_sample, cache_k, cache_v, state_mlstm_c, state_mlstm_n, state_mlstm_m, state_rwkv, state_rwkv_shift, state_gla, page_table, ffn1_norm, ffn1_w_in, ffn1_w_out, mix_norm, w_in, w_out, ffn2_norm, ffn2_w_in, ffn2_w_out, mlstm_gate_b, mlstm_norm_w, rwkv_mu, rwkv_w0, rwkv_w2, rwkv_a0, rwkv_a2, rwkv_g2, rwkv_k_k, rwkv_k_a, rwkv_r_k, rwkv_ln_w, rwkv_ln_b, gla_gk_up, gla_gk_b, gla_norm_w, moba_q_norm, moba_k_norm):
    nb, t, d = x_prompt.shape
    ns, ts, _ = x_sample.shape
    depth = w_in.shape[0]
    assert nb == 2 and ns % 2 == 0 and t % MOBA_BLOCK == 0 and ts <= 8
    npc = t // CH
    n_prompt = nb * t
    rows = n_prompt + ns * CH
    past_len = page_table.shape[1] * cache_k.shape[2]
    kw = dict(npc=npc, n_sample=ns, t_s=ts)
    tm = 512 if rows % 512 == 0 else 256

    x = jnp.concatenate([x_prompt.reshape(n_prompt, d),
                         jnp.pad(x_sample, ((0, 0), (0, CH - ts), (0, 0))).reshape(ns * CH, d)], axis=0)
    pos = jnp.concatenate([jnp.tile(jnp.arange(t), nb), jnp.tile(past_len + jnp.arange(CH), ns)])
    inv = ROPE_THETA ** (-(jnp.arange(A_ROT // 2, dtype=F32) * 2.0 / A_ROT))
    ang = pos.astype(F32)[:, None] * inv[None, :]
    one = jnp.ones((rows, A_HEAD - A_ROT), F32)
    cos_h = jnp.concatenate([jnp.cos(ang), jnp.cos(ang), one], axis=1)
    sin_h = jnp.concatenate([-jnp.sin(ang), jnp.sin(ang), 0.0 * one], axis=1)
    cos_t = jnp.concatenate([cos_h, cos_h], axis=1)
    sin_t = jnp.concatenate([sin_h, sin_h], axis=1)

    w_in_p = _pack_w_in(w_in)
    w_out_b = w_out.astype(BF16)
    f1_in, f1_out = ffn1_w_in.astype(BF16), ffn1_w_out.astype(BF16)
    f2_in, f2_out = ffn2_w_in.astype(BF16), ffn2_w_out.astype(BF16)

    def with_prompt_zeros(st):
        return jnp.concatenate([jnp.zeros((nb,) + st.shape[1:], F32), st], axis=0)

    def sample_rows(a, n):
        return a[n_prompt:].reshape(ns, CH, a.shape[-1])[:, :n]

    def heads_major(a):
        return a.reshape(ns, 8, A_HEADS, A_HEAD).transpose(0, 2, 1, 3)

    per_layer = []
    for l in range(depth):
        x = _ffn(x, ffn1_norm[l], f1_in[l], f1_out[l], tm=tm)
        p = _proj(x, mix_norm[l], w_in_p[l], tm=tm)
        p3 = p.reshape(rows // CH, CH, NPK)

        ce = jnp.concatenate([state_mlstm_c[l], state_mlstm_n[l][..., None],
                              jnp.zeros(state_mlstm_n[l].shape + (LANES - 1,), F32)], axis=-1)
        ce0 = with_prompt_zeros(_place_half(ce, M_HEADS, -2))
        m0 = jnp.zeros((nb + ns, 8, LANES), F32).at[nb:, 0, :M_HEADS].set(state_mlstm_m[l])
        ym_p, ym_s, ce_a, ce_b, m_a, m_b = _mlstm(p3, ce0, m0, mlstm_gate_b[l], mlstm_norm_w[l], **kw)
        ce_p, ce_s = (_take_half(c, M_HEADS, -2) for c in _merge_state(ce_a, ce_b))
        mm_p, mm_s = (m[:, 0, :M_HEADS] for m in _merge_state(m_a, m_b))

        st_t = jnp.swapaxes(state_rwkv[l], -1, -2)
        w0s = with_prompt_zeros(_place_half(_place_half(st_t, R_HEADS, -2), R_HEADS, -1))
        sh0 = jnp.zeros((nb + ns, 8, GW), F32).at[nb:, :, :P_R].set(state_rwkv_shift[l][:, None, :])
        lw = dict(rwkv_mu=rwkv_mu[l], rwkv_w0=rwkv_w0[l], rwkv_w2=rwkv_w2[l], rwkv_a0=rwkv_a0[l],
                  rwkv_a2=rwkv_a2[l], rwkv_g2=rwkv_g2[l], rwkv_k_k=rwkv_k_k[l], rwkv_k_a=rwkv_k_a[l],
                  rwkv_r_k=rwkv_r_k[l].reshape(-1), rwkv_ln_w=rwkv_ln_w[l], rwkv_ln_b=rwkv_ln_b[l])
        yr_p, yr_s, w_a, w_b, sh_a, sh_b = _rwkv(p3, w0s, sh0, lw, **kw)
        rw_p, rw_s = (jnp.swapaxes(_take_half(_take_half(w, R_HEADS, -2), R_HEADS, -1), -1, -2)
                      for w in _merge_state(w_a, w_b))
        sh_p, sh_s = (sh[:, 0, :P_R] for sh in _merge_state(sh_a, sh_b))

        s0 = with_prompt_zeros(_place_half(state_gla[l], G_HEADS, -2))
        yg_p, yg_s, g_a, g_b = _gla(p3, s0, gla_gk_up[l], gla_gk_b[l], gla_norm_w[l], **kw)
        gl_p, gl_s = (_take_half(gs, G_HEADS, -2) for gs in _merge_state(g_a, g_b))

        q_all, k_all, kb, vb, km = _moba_prep(p, cos_t, sin_t, moba_q_norm[l], moba_k_norm[l])
        v_all = p[:, 3 * GW + 2 * GROUP:3 * GW + 3 * GROUP]
        nbk = t // MOBA_BLOCK
        ya_p = _moba_prompt(q_all, kb, vb, km[:nb * nbk].reshape(nb, nbk, GROUP), nbatch=nb, t=t)
        q_s = sample_rows(q_all, 8)
        sel = _moba_gate(cache_k, l, page_table, q_s[:, :ts].reshape(ns, ts, A_HEADS, A_HEAD))[..., 0]
        sel = sel.reshape(ns, ts, MOBA_TOPK, A_HEADS).transpose(0, 3, 1, 2)
        per_blk = MOBA_BLOCK // cache_k.shape[2]
        pidx = per_blk * sel[..., None] + jnp.arange(per_blk)
        pages = page_table[jnp.arange(ns).reshape(ns, 1, 1, 1, 1), pidx].reshape(-1).astype(jnp.int32)
        o_s = _moba_sample(pages, heads_major(q_s), heads_major(sample_rows(k_all, 8)),
                           heads_major(sample_rows(v_all, 8)), cache_k, cache_v, l, tq=ts)
        ya_s = jnp.pad(o_s.transpose(0, 2, 1, 3).reshape(ns, 8, GROUP), ((0, 0), (0, CH - 8), (0, 0)))
        ya = jnp.concatenate([ya_p, ya_s.reshape(ns * CH, GROUP)], axis=0)

        x = _outproj(x, _merge_y(ym_p, ym_s), _merge_y(yr_p, yr_s), _merge_y(yg_p, yg_s), ya, w_out_b[l], tm=tm)
        x = _ffn(x, ffn2_norm[l], f2_in[l], f2_out[l], tm=tm)

        shp_p, shp_s = (nb, t, A_HEADS, A_HEAD), (ns, ts, A_HEADS, A_HEAD)
        per_layer.append((
            k_all[:n_prompt].reshape(shp_p), v_all[:n_prompt].reshape(shp_p),
            sample_rows(k_all, ts).reshape(shp_s), sample_rows(v_all, ts).reshape(shp_s),
            ce_p[..., :M_DV], ce_s[..., :M_DV], ce_p[..., M_DV], ce_s[..., M_DV], mm_p, mm_s,
            rw_p, rw_s, sh_p, sh_s, gl_p, gl_s))

    y_p = x[:n_prompt].reshape(nb, t, d)
    y_s = x[n_prompt:].reshape(ns, CH, d)[:, :ts]
    stacked = tuple(jnp.stack([lay[i] for lay in per_layer]) for i in range(len(per_layer[0])))
    return (y_p, y_s) + stacked
```

```python
import functools

import jax
import jax.numpy as jnp
import numpy as np
from jax import lax
from jax.experimental import pallas as pl
from jax.experimental.pallas import tpu as pltpu

F32 = jnp.float32
BF16 = jnp.bfloat16
HI = lax.Precision.HIGHEST

D_MODEL = 2048
GROUP = D_MODEL // 4
M_HEADS = 4
M_DV = GROUP // M_HEADS
M_DK = M_DV // 2
GATE_CAP = 15.0
R_HEAD = 64
R_HEADS = GROUP // R_HEAD
R_LORA = 32
R_GN_EPS = 64e-5
G_HEADS = 4
G_DV = GROUP // G_HEADS
G_DK = G_DV // 2
G_LORA = 16
G_TAU = 16.0
A_HEAD = 64
A_HEADS = GROUP // A_HEAD
A_ROT = A_HEAD // 4
ROPE_THETA = 500000.0
MOBA_BLOCK = 256
MOBA_TOPK = 3
RMS_EPS = 1e-6
P_M = 2 * M_HEADS * M_DK + 2 * GROUP + 2 * M_HEADS
P_R = 3 * GROUP + 3 * R_LORA
P_G = 2 * G_HEADS * G_DK + 2 * GROUP + G_LORA
P_A = 3 * GROUP

CH = 64
GW = 1664
NPK = 4 * GW
NEG = -1e30
LANES = 128
VMEM_LIMIT = 48 * 1024 * 1024


def _dot(a, b, prec=None):
    return jnp.dot(a, b, preferred_element_type=F32, precision=prec)


def _dot_nt(a, b, prec=None):
    return lax.dot_general(a, b, (((1,), (1,)), ((), ())), preferred_element_type=F32, precision=prec)


def _dot_tn(a, b, prec=None):
    return lax.dot_general(a, b, (((0,), (0,)), ((), ())), preferred_element_type=F32, precision=prec)


def _bf(x):
    return x.astype(BF16)


def _dotb(a, b):
    return _dot(_bf(a), _bf(b))


def _dotb_nt(a, b):
    return _dot_nt(_bf(a), _bf(b))


def _dotb_tn(a, b):
    return _dot_tn(_bf(a), _bf(b))


def _split2(x):
    hi = x.astype(BF16)
    return hi, (x - hi.astype(F32)).astype(BF16)


def _dot_sel_rhs(a, sel):
    hi, lo = _split2(a)
    sel = _bf(sel)
    return _dot(hi, sel) + _dot(lo, sel)


def _dot_sel_lhs(sel, b):
    hi, lo = _split2(b)
    sel = _bf(sel)
    return _dot(sel, hi) + _dot(sel, lo)


def _row_to_col(row, eye):
    return jnp.sum(eye * row, axis=1, keepdims=True)


def _iota(shape, dim):
    return lax.broadcasted_iota(jnp.int32, shape, dim)


def _sigmoid(x):
    return 1.0 / (1.0 + jnp.exp(-x))


def _softplus(x):
    return jnp.maximum(x, 0.0) + jnp.log(1.0 + jnp.exp(-jnp.abs(x)))


def _log_sigmoid(x):
    return -_softplus(-x)


def _tril(n):
    return (_iota((n, n), 1) <= _iota((n, n), 0)).astype(F32)


def _cparams(sem, vmem=VMEM_LIMIT):
    return pltpu.CompilerParams(dimension_semantics=sem, vmem_limit_bytes=vmem)


def _rms_rows(x, w):
    return x * lax.rsqrt(jnp.mean(x * x, axis=-1, keepdims=True) + RMS_EPS) * w


def _ffn_kernel(x_ref, nw_ref, wg_ref, wu_ref, wo_ref, o_ref, h_ref, acc_ref):
    j = pl.program_id(1)

    @pl.when(j == 0)
    def _():
        h_ref[...] = _rms_rows(x_ref[...], nw_ref[...]).astype(BF16)
        acc_ref[...] = jnp.zeros_like(acc_ref)

    h = h_ref[...]
    gate = _dot(h, wg_ref[...])
    up = _dot(h, wu_ref[...])
    act = (gate * _sigmoid(gate) * up).astype(BF16)
    acc_ref[...] += _dot(act, wo_ref[...])

    @pl.when(j == pl.num_programs(1) - 1)
    def _():
        o_ref[...] = x_ref[...] + 0.5 * acc_ref[...]


def _ffn(x, norm_w, w_in, w_out, *, tm, tf=512):
    rows, d = x.shape
    dff = w_out.shape[0]
    nj = dff // tf
    return pl.pallas_call(
        _ffn_kernel,
        grid=(rows // tm, nj),
        in_specs=[
            pl.BlockSpec((tm, d), lambda i, j: (i, 0)),
            pl.BlockSpec((1, d), lambda i, j: (0, 0)),
            pl.BlockSpec((d, tf), lambda i, j: (0, j)),
            pl.BlockSpec((d, tf), lambda i, j: (0, j + nj)),
            pl.BlockSpec((tf, d), lambda i, j: (j, 0)),
        ],
        out_specs=pl.BlockSpec((tm, d), lambda i, j: (i, 0)),
        out_shape=jax.ShapeDtypeStruct((rows, d), F32),
        scratch_shapes=[pltpu.VMEM((tm, d), BF16), pltpu.VMEM((tm, d), F32)],
        compiler_params=_cparams(("parallel", "arbitrary")),
        name="ffn",
    )(x, norm_w.reshape(1, d), w_in, w_in, w_out)


def _proj_kernel(x_ref, nw_ref, w_ref, o_ref, h_ref):
    @pl.when(pl.program_id(1) == 0)
    def _():
        h_ref[...] = _rms_rows(x_ref[...], nw_ref[...]).astype(BF16)

    o_ref[...] = _dot(h_ref[...], w_ref[...])


def _proj(x, norm_w, w, *, tm, tn=512):
    rows, d = x.shape
    n = w.shape[1]
    return pl.pallas_call(
        _proj_kernel,
        grid=(rows // tm, n // tn),
        in_specs=[
            pl.BlockSpec((tm, d), lambda i, j: (i, 0)),
            pl.BlockSpec((1, d), lambda i, j: (0, 0)),
            pl.BlockSpec((d, tn), lambda i, j: (0, j)),
        ],
        out_specs=pl.BlockSpec((tm, tn), lambda i, j: (i, j)),
        out_shape=jax.ShapeDtypeStruct((rows, n), F32),
        scratch_shapes=[pltpu.VMEM((tm, d), BF16)],
        compiler_params=_cparams(("parallel", "arbitrary")),
        name="proj",
    )(x, norm_w.reshape(1, d), w)


def _outproj_kernel(x_ref, ym_ref, yr_ref, yg_ref, ya_ref, w_ref, o_ref):
    acc = x_ref[...]
    for gi, y_ref in enumerate((ym_ref, yr_ref, yg_ref, ya_ref)):
        acc = acc + _dot(y_ref[...].astype(BF16), w_ref[gi * GROUP:(gi + 1) * GROUP, :])
    o_ref[...] = acc


def _outproj(x, ym, yr, yg, ya, w, *, tm):
    rows, d = x.shape
    yspec = pl.BlockSpec((tm, GROUP), lambda i: (i, 0))
    return pl.pallas_call(
        _outproj_kernel,
        grid=(rows // tm,),
        in_specs=[pl.BlockSpec((tm, d), lambda i: (i, 0)), yspec, yspec, yspec, yspec,
                  pl.BlockSpec((d, d), lambda i: (0, 0))],
        out_specs=pl.BlockSpec((tm, d), lambda i: (i, 0)),
        out_shape=jax.ShapeDtypeStruct((rows, d), F32),
        compiler_params=_cparams(("parallel",)),
        name="outproj",
    )(x, ym, yr, yg, ya, w)


def _const_spec(shape):
    return pl.BlockSpec(shape, lambda g: (0,) * len(shape))


def _mixer_specs(npc, group, state_blocks):
    def chunk_idx(s):
        return lambda g: (jnp.where(g < npc, s * npc + g, 2 * npc + 2 * (g - npc) + s), 0, group)

    p_specs = [pl.BlockSpec((None, CH, GW), chunk_idx(s)) for s in (0, 1)]
    in_state, out_state = [], []
    for blk in state_blocks:
        zeros = (0,) * len(blk)
        for s in (0, 1):
            in_state.append(pl.BlockSpec(
                (None,) + blk, lambda g, s=s, z=zeros: (jnp.where(g < npc, s, 2 + 2 * (g - npc) + s),) + z))
            out_state.append(pl.BlockSpec(
                (None,) + blk, lambda g, z=zeros: (jnp.where(g < npc, 0, 1 + g - npc),) + z))
    y_specs = [pl.BlockSpec((2, None, CH, GROUP), lambda g: (0, jnp.minimum(g, npc - 1), 0, 0)),
               pl.BlockSpec((None, 2, CH, GROUP), lambda g: (jnp.maximum(g - npc, 0), 0, 0, 0))]
    return p_specs, in_state, out_state, y_specs


def _mixer_out_shapes(npc, n_sample, state_blocks):
    ys = [jax.ShapeDtypeStruct((2, npc, CH, GROUP), F32), jax.ShapeDtypeStruct((n_sample // 2, 2, CH, GROUP), F32)]
    st = []
    for blk in state_blocks:
        st += [jax.ShapeDtypeStruct((1 + n_sample // 2,) + blk, F32)] * 2
    return ys, st


def _step_info(npc, t_s):
    g = pl.program_id(0)
    first = jnp.logical_or(g == 0, g >= npc)
    last = g >= npc - 1
    tlen = jnp.where(g < npc, CH, t_s)
    return g, first, last, tlen


def _write_y(g, npc, yp_ref, ys_ref, s, val):
    @pl.when(g < npc)
    def _():
        yp_ref[s] = val

    @pl.when(g >= npc)
    def _():
        ys_ref[s] = val


def _merge_y(yp, ys):
    return jnp.concatenate([yp.reshape(-1, GROUP), ys.reshape(-1, GROUP)], axis=0)


def _merge_state(a, b):
    prompt = jnp.stack([a[0], b[0]])
    sample = jnp.stack([a[1:], b[1:]], axis=1).reshape((-1,) + a.shape[1:])
    return prompt, sample


def _mlstm_kernel(pa_ref, pb_ref, cea_ref, ceb_ref, ma_ref, mb_ref, gb_ref, nw_ref,
                  yp_ref, ys_ref, ceoa_ref, ceob_ref, moa_ref, mob_ref, ce_sc, m_sc, *, npc, t_s):
    g, first, last, tlen = _step_info(npc, t_s)
    slots = ((pa_ref, cea_ref, ma_ref, ceoa_ref, moa_ref), (pb_ref, ceb_ref, mb_ref, ceob_ref, mob_ref))
    row = _iota((CH, LANES), 0)
    lane = _iota((CH, LANES), 1)
    tril = _tril(CH)
    causal = _iota((CH, CH), 1) <= _iota((CH, CH), 0)
    e0 = (lane == 0).astype(F32)
    valid = row < tlen

    @pl.when(first)
    def _():
        for s, (_, ce0_ref, m0_ref, _, _) in enumerate(slots):
            ce_sc[s] = ce0_ref[...]
            m_sc[s] = m0_ref[...]

    units = []
    for s, (p_ref, _, _, _, _) in enumerate(slots):
        gates = p_ref[:, 1536:1664] + gb_ref[...]
        gates = GATE_CAP * jnp.tanh(gates / GATE_CAP)
        ig = jnp.where(valid, gates, NEG)
        lf = jnp.where(valid, _log_sigmoid(gates), 0.0)
        b_col = _dot_sel_lhs(tril, lf)
        ig_t = ig.T
        lf_hi, lf_lo = _split2(lf.T[0:8])
        b_row = _dot_nt(lf_hi, _bf(tril)) + _dot_nt(lf_lo, _bf(tril))
        for h in range(M_HEADS):
            pair, half = h // 2, h % 2
            hm = (lane // 64) == half
            b_c = b_col[:, 4 + h:5 + h]
            m_h = m_sc[s, 0:1, h:h + 1]
            dmat = jnp.where(causal, b_c - b_row[4 + h:5 + h, :] + ig_t[h:h + 1, :], NEG)
            m_inter = b_c + m_h
            mt = jnp.maximum(m_inter, jnp.max(dmat, axis=1, keepdims=True))
            m_new = mt[CH - 1:CH, :]
            b_last = b_c[CH - 1:CH, :]
            units.append(dict(
                s=s, h=h, p_ref=p_ref, mt=mt, m_new=m_new, dexp=jnp.exp(dmat - mt), s_inter=jnp.exp(m_inter - mt),
                carry=jnp.exp(b_last + m_h - m_new), ws=jnp.exp(b_last - b_c + ig[:, h:h + 1] - m_new),
                qm=jnp.where(hm, p_ref[:, 128 * pair:128 * pair + 128], 0.0),
                km=jnp.where(hm, p_ref[:, 256 + 128 * pair:256 + 128 * pair + 128], 0.0) * (M_DK ** -0.5),
                v_ext=jnp.concatenate([p_ref[:, 512 + 128 * h:512 + 128 * h + 128], e0], axis=1)))
    ces = [ce_sc[u['s'], u['h']] for u in units]
    qks = [_dotb_nt(u['qm'], u['km']) for u in units]
    qcs = [_dotb(u['qm'], ce) for u, ce in zip(units, ces)]
    nds = [u['s_inter'] * qc + _dotb(u['dexp'] * qk, u['v_ext']) for u, qk, qc in zip(units, qks, qcs)]
    for u, ce in zip(units, ces):
        ce_sc[u['s'], u['h']] = u['carry'] * ce + _dotb_tn(u['km'], u['ws'] * u['v_ext'])
        m_sc[u['s'], 0:1, u['h']:u['h'] + 1] = u['m_new']
    for s, (p_ref, _, _, ceo_ref, mo_ref) in enumerate(slots):
        ys = []
        for h in range(M_HEADS):
            u, nd = units[s * M_HEADS + h], nds[s * M_HEADS + h]
            hh = nd[:, 0:128] / jnp.maximum(jnp.abs(nd[:, 128:129]), jnp.exp(-u['mt']))
            hn = hh * lax.rsqrt(jnp.mean(hh * hh, axis=-1, keepdims=True) + RMS_EPS)
            hn = hn * nw_ref[:, 128 * h:128 * h + 128]
            ys.append(hn * _sigmoid(p_ref[:, 1024 + 128 * h:1024 + 128 * h + 128]))
        _write_y(g, npc, yp_ref, ys_ref, s, jnp.concatenate(ys, axis=1))

        @pl.when(last)
        def _():
            ceo_ref[...] = ce_sc[s]
            mo_ref[...] = m_sc[s]


def _mlstm(p3, ce0, m0, gate_b, norm_w, *, npc, n_sample, t_s):
    blocks = [(M_HEADS, 128, 256), (8, LANES)]
    p_specs, in_state, out_state, y_specs = _mixer_specs(npc, 0, blocks)
    y_shapes, st_shapes = _mixer_out_shapes(npc, n_sample, blocks)
    gb = jnp.zeros((1, LANES), F32).at[0, :2 * M_HEADS].set(gate_b)
    return pl.pallas_call(
        functools.partial(_mlstm_kernel, npc=npc, t_s=t_s),
        grid=(npc + n_sample // 2,),
        in_specs=p_specs + in_state + [_const_spec((1, LANES)), _const_spec((1, GROUP))],
        out_specs=y_specs + out_state,
        out_shape=y_shapes + st_shapes,
        scratch_shapes=[pltpu.VMEM((2, M_HEADS, 128, 256), F32), pltpu.VMEM((2, 8, LANES), F32)],
        compiler_params=_cparams(("arbitrary",)),
        name="mlstm",
    )(p3, p3, ce0, ce0, m0, m0, gb, norm_w.reshape(1, GROUP))


def _gla_kernel(pa_ref, pb_ref, sa_ref, sb_ref, up_ref, gkb_ref, nw_ref,
                yp_ref, ys_ref, soa_ref, sob_ref, s_sc, k_sc, b_sc, *, npc, t_s):
    g, first, last, tlen = _step_info(npc, t_s)
    slots = ((pa_ref, sa_ref, soa_ref), (pb_ref, sb_ref, sob_ref))
    row = _iota((CH, LANES), 0)
    lane = _iota((CH, LANES), 1)
    tril = _tril(CH)
    r128 = _iota((LANES, LANES), 0)
    c128 = _iota((LANES, LANES), 1)
    half_ones = ((r128 // 64) == (c128 // 64)).astype(F32)
    eye = (r128 == c128).astype(F32)
    valid = row < tlen

    @pl.when(first)
    def _():
        for s, (_, s0_ref, _) in enumerate(slots):
            s_sc[s] = s0_ref[...]

    pairs = []
    for s, (p_ref, _, _) in enumerate(slots):
        z = _dotb(p_ref[:, 1536:1664], up_ref[...]) + gkb_ref[...]
        la = _log_sigmoid(z) / G_TAU
        la = jnp.where(jnp.concatenate([valid, valid], axis=1), la, 0.0)
        b = _dot_sel_lhs(tril, la)
        for pair in range(2):
            sl = slice(128 * pair, 128 * pair + 128)
            kp = jnp.where(valid, p_ref[:, 256 + 128 * pair:256 + 128 * pair + 128], 0.0)
            bp = b[:, sl]
            k_sc[s, pair] = kp
            b_sc[s, pair] = bp
            pairs.append(dict(s=s, pair=pair, p_ref=p_ref, qp=p_ref[:, sl] * (G_DK ** -0.5), kp=kp, bp=bp,
                              bl=bp[CH - 1:CH, :]))

    def body(ti, accs):
        r0 = pl.multiple_of(ti * 8, 8)
        sums = []
        for u in pairs:
            k8 = k_sc[u['s'], u['pair'], pl.ds(r0, 8), :]
            b8 = b_sc[u['s'], u['pair'], pl.ds(r0, 8), :]
            e = [u['qp'] * k8[j:j + 1, :] * jnp.exp(jnp.minimum(u['bp'] - b8[j:j + 1, :], 0.0)) for j in range(8)]
            sums.append(_dot_sel_rhs(jnp.concatenate(e, axis=0), half_ones))
        out = []
        for acc, sm in zip(accs, sums):
            for j in range(8):
                acc = jnp.where((lane % 64) == r0 + j, sm[CH * j:CH * (j + 1)], acc)
            out.append(acc)
        return tuple(out)

    atts = lax.fori_loop(0, CH // 8, body, tuple(jnp.zeros((CH, LANES), F32) for _ in pairs))
    units = []
    for u, att in zip(pairs, atts):
        att = jnp.where((lane % 64) <= row, att, 0.0)
        decay = _row_to_col(jnp.exp(u['bl']), eye)
        for half in range(2):
            h = 2 * u['pair'] + half
            hm = (lane // 64) == half
            v_h = u['p_ref'][:, 512 + 128 * h:512 + 128 * h + 128]
            units.append(dict(s=u['s'], h=h, v_h=v_h, decay=decay, p_ref=u['p_ref'],
                              qe=jnp.where(hm, u['qp'] * jnp.exp(u['bp']), 0.0), att=jnp.where(hm, att, 0.0),
                              khat=jnp.where(hm, u['kp'] * jnp.exp(u['bl'] - u['bp']), 0.0)))
    sts = [s_sc[u['s'], u['h']] for u in units]
    outs = [_dotb(u['qe'], st) + _dotb(u['att'], jnp.concatenate([u['v_h'], u['v_h']], axis=0))
            for u, st in zip(units, sts)]
    for u, st in zip(units, sts):
        s_sc[u['s'], u['h']] = u['decay'] * st + _dotb_tn(u['khat'], u['v_h'])
    for s, (p_ref, _, so_ref) in enumerate(slots):
        ys = []
        for h in range(G_HEADS):
            o = outs[s * G_HEADS + h]
            on = o * lax.rsqrt(jnp.mean(o * o, axis=-1, keepdims=True) + RMS_EPS) * nw_ref[...]
            gt = p_ref[:, 1024 + 128 * h:1024 + 128 * h + 128]
            ys.append(on * gt * _sigmoid(gt))
        _write_y(g, npc, yp_ref, ys_ref, s, jnp.concatenate(ys, axis=1))

        @pl.when(last)
        def _():
            so_ref[...] = s_sc[s]


def _gla(p3, s0, gk_up, gk_b, norm_w, *, npc, n_sample, t_s):
    blocks = [(G_HEADS, 128, 128)]
    p_specs, in_state, out_state, y_specs = _mixer_specs(npc, 2, blocks)
    y_shapes, st_shapes = _mixer_out_shapes(npc, n_sample, blocks)
    up = jnp.zeros((LANES, G_HEADS * G_DK), F32).at[:G_LORA].set(gk_up)
    return pl.pallas_call(
        functools.partial(_gla_kernel, npc=npc, t_s=t_s),
        grid=(npc + n_sample // 2,),
        in_specs=p_specs + in_state + [_const_spec((LANES, 256)), _const_spec((1, 256)), _const_spec((1, G_DV))],
        out_specs=y_specs + out_state,
        out_shape=y_shapes + st_shapes,
        scratch_shapes=[pltpu.VMEM((2, G_HEADS, 128, 128), F32), pltpu.VMEM((2, 2, CH, LANES), F32),
                        pltpu.VMEM((2, 2, CH, LANES), F32)],
        compiler_params=_cparams(("arbitrary",)),
        name="gla",
    )(p3, p3, s0, s0, up, gk_b.reshape(1, -1), norm_w.reshape(1, G_DV))


def _rwkv_kernel(pa_ref, pb_ref, wa_ref, wb_ref, sha_ref, shb_ref, mu_ref, w0_ref, w2_ref, a0_ref, a2_ref,
                 g2_ref, kk_ref, ka_ref, rk_ref, lnw_ref, lnb_ref, bones_ref,
                 yp_ref, ys_ref, woa_ref, wob_ref, shoa_ref, shob_ref, w_sc, sh_sc, *, npc, t_s):
    g, first, last, tlen = _step_info(npc, t_s)
    slots = ((pa_ref, wa_ref, sha_ref, woa_ref, shoa_ref), (pb_ref, wb_ref, shb_ref, wob_ref, shob_ref))
    lane = _iota((CH, LANES), 1)
    tril = _tril(CH)
    r64, c64 = _iota((CH, CH), 0), _iota((CH, CH), 1)
    lower_strict = c64 < r64
    lower = c64 <= r64
    eye64 = (r64 == c64).astype(F32)
    eye128 = (_iota((LANES, LANES), 0) == _iota((LANES, LANES), 1)).astype(F32)
    bones = bones_ref[...]
    valid = _iota((CH, GROUP), 0) < tlen

    @pl.when(first)
    def _():
        for s, (_, w0s_ref, sh0_ref, _, _) in enumerate(slots):
            w_sc[s] = w0s_ref[...]
            sh_sc[s] = sh0_ref[...]

    units, tails = [], []
    for s, (p_ref, _, _, _, _) in enumerate(slots):
        pf = p_ref[...]
        prev = jnp.where(_iota((CH, GW), 0) == 0, sh_sc[s, 0:1, :], pltpu.roll(pf, 1, 0))
        xs = pf + (prev - pf) * mu_ref[...]
        r = xs[:, 0:512]
        k = xs[:, 512:1024]
        v = xs[:, 1024:1536]
        lo = xs[:, 1536:1664]
        wraw = -_softplus(-(w0_ref[...] + _dotb(jnp.tanh(lo), w2_ref[...]))) - 0.5
        logw = jnp.where(valid, -jnp.exp(wraw), 0.0)
        a = _sigmoid(a0_ref[...] + _dotb(lo, a2_ref[...]))
        gg = _dotb(_sigmoid(lo), g2_ref[...])
        kk = k * kk_ref[...]
        kk = kk / jnp.maximum(jnp.sqrt(_dot_sel_rhs(kk * kk, bones)), 1e-12)
        k2 = k * (1.0 + (a - 1.0) * ka_ref[...])
        k2m = jnp.where(valid, k2, 0.0)
        bv = jnp.where(valid, kk * a, 0.0)
        lg = _dot_sel_lhs(tril, logw)
        lgl = lg[CH - 1:CH, :]
        e_out = jnp.exp(-lg)
        e_end = jnp.exp(lgl - lg)
        at = -kk * jnp.exp(lg - logw)
        rt = r * jnp.exp(lg)
        bt = bv * e_out
        kt = k2m * e_out
        bh = bv * e_end
        kh = k2m * e_end
        gl = jnp.exp(lgl)
        tails.append((r, k2, v, gg))
        for pair in range(R_HEADS // 2):
            sl = slice(128 * pair, 128 * pair + 128)
            decay = _row_to_col(gl[:, sl], eye128)
            for half in range(2):
                hm = (lane // 64) == half
                units.append(dict(
                    s=s, h=2 * pair + half, vp=v[:, sl], decay=decay, bt=bt[:, sl], kt=kt[:, sl],
                    ar=jnp.concatenate([jnp.where(hm, at[:, sl], 0.0), jnp.where(hm, rt[:, sl], 0.0)], axis=0),
                    lhs=jnp.concatenate([jnp.where(hm, bh[:, sl], 0.0), jnp.where(hm, kh[:, sl], 0.0)], axis=0)))
    gbs = [_dotb_nt(u['ar'], u['bt']) for u in units]
    gks = [_dotb_nt(u['ar'], u['kt']) for u in units]
    mxs = [jnp.where(lower_strict, gb[0:CH], 0.0) for gb in gbs]
    tms = [eye64 + mx for mx in mxs]
    for _ in range(5):
        mxs = [_dotb(mx, mx) for mx in mxs]
        tms = [tm + _dotb(tm, mx) for tm, mx in zip(tms, mxs)]
    wsts = [w_sc[u['s'], u['h']] for u in units]
    arws = [_dotb(u['ar'], wst) for u, wst in zip(units, wsts)]
    inner = [_dotb(jnp.where(lower_strict, gk[0:CH], 0.0), u['vp']) for u, gk in zip(units, gks)]
    pms = [_dotb(tm, arw[0:CH] + inn) for tm, arw, inn in zip(tms, arws, inner)]
    yhs = [arw[CH:2 * CH] + _dotb(jnp.where(lower, gb[CH:2 * CH], 0.0), pm)
           + _dotb(jnp.where(lower, gk[CH:2 * CH], 0.0), u['vp'])
           for u, arw, gb, gk, pm in zip(units, arws, gbs, gks, pms)]
    for u, wst, pm in zip(units, wsts, pms):
        w_sc[u['s'], u['h']] = u['decay'] * wst + _dotb_tn(u['lhs'], jnp.concatenate([pm, u['vp']], axis=0))
    for s, (p_ref, _, _, wo_ref, sho_ref) in enumerate(slots):
        r, k2, v, gg = tails[s]
        yh = yhs[s * R_HEADS:(s + 1) * R_HEADS]
        y = jnp.concatenate([jnp.where(lane < 64, yh[2 * pr], yh[2 * pr + 1]) for pr in range(R_HEADS // 2)], axis=1)
        mean = _dot_sel_rhs(y, bones) * (1.0 / R_HEAD)
        d = y - mean
        var = _dot_sel_rhs(d * d, bones) * (1.0 / R_HEAD)
        yn = d * lax.rsqrt(var + R_GN_EPS) * lnw_ref[...] + lnb_ref[...]
        bonus = _dot_sel_rhs(r * k2 * rk_ref[...], bones) * v
        _write_y(g, npc, yp_ref, ys_ref, s, (yn + bonus) * gg)
        last_row = jnp.where(g < npc, p_ref[CH - 1:CH, :], p_ref[t_s - 1:t_s, :])
        sh_sc[s] = jnp.broadcast_to(last_row, (8, GW))

        @pl.when(last)
        def _():
            wo_ref[...] = w_sc[s]
            sho_ref[...] = sh_sc[s]


def _head_block_ones(width, head):
    hid = np.arange(width) // head
    return jnp.asarray((hid[:, None] == hid[None, :]).astype(np.float32))


def _rwkv(p3, w0s, sh0, lw, *, npc, n_sample, t_s):
    blocks = [(R_HEADS, 128, 128), (8, GW)]
    p_specs, in_state, out_state, y_specs = _mixer_specs(npc, 1, blocks)
    y_shapes, st_shapes = _mixer_out_shapes(npc, n_sample, blocks)

    def lora(w, off):
        return jnp.zeros((LANES, GROUP), F32).at[off:off + R_LORA].set(w)

    def row(t):
        return t.reshape(1, GROUP)

    mu = jnp.zeros((1, GW), F32).at[0, :P_R].set(lw['rwkv_mu'])
    consts = [mu, row(lw['rwkv_w0']), lora(lw['rwkv_w2'], 0), row(lw['rwkv_a0']), lora(lw['rwkv_a2'], R_LORA),
              lora(lw['rwkv_g2'], 2 * R_LORA), row(lw['rwkv_k_k']), row(lw['rwkv_k_a']), row(lw['rwkv_r_k']),
              row(lw['rwkv_ln_w']), row(lw['rwkv_ln_b']), _head_block_ones(GROUP, R_HEAD)]
    return pl.pallas_call(
        functools.partial(_rwkv_kernel, npc=npc, t_s=t_s),
        grid=(npc + n_sample // 2,),
        in_specs=p_specs + in_state + [_const_spec(c.shape) for c in consts],
        out_specs=y_specs + out_state,
        out_shape=y_shapes + st_shapes,
        scratch_shapes=[pltpu.VMEM((2, R_HEADS, 128, 128), F32), pltpu.VMEM((2, 8, GW), F32)],
        compiler_params=_cparams(("arbitrary",)),
        name="rwkv",
    )(p3, p3, w0s, w0s, sh0, sh0, *consts)


def _moba_prep_kernel(p_ref, cos_ref, sin_ref, qn_ref, kn_ref, bones_ref, q_ref, k_ref, kb_ref, vt_ref, km_ref):
    bones = bones_ref[...]
    cos = jnp.concatenate([cos_ref[...]] * 4, axis=1)
    sin = jnp.concatenate([sin_ref[...]] * 4, axis=1)
    lane = _iota(cos.shape, 1)
    low = (lane % A_HEAD) < (A_ROT // 2)

    def norm_rope(x, w):
        ms = _dot_sel_rhs(x * x, bones) * (1.0 / A_HEAD)
        xn = x * lax.rsqrt(ms + RMS_EPS) * w
        partner = jnp.where(low, pltpu.roll(xn, GROUP - A_ROT // 2, 1), pltpu.roll(xn, A_ROT // 2, 1))
        return xn * cos + partner * sin

    q_ref[...] = norm_rope(p_ref[:, 0:512], qn_ref[...])
    kr = norm_rope(p_ref[:, 512:1024], kn_ref[...])
    k_ref[...] = kr
    kb_ref[...] = kr.astype(BF16)
    vt_ref[...] = p_ref[:, 1024:1536].T.astype(BF16)
    km_ref[...] = jnp.mean(kr, axis=0, keepdims=True)


def _moba_prep(p, cos_t, sin_t, q_norm, k_norm):
    rows = p.shape[0]
    nt = rows // MOBA_BLOCK
    rspec = pl.BlockSpec((MOBA_BLOCK, GROUP), lambda i: (i, 0))
    tspec = pl.BlockSpec((MOBA_BLOCK, LANES), lambda i: (i, 0))
    cspec = pl.BlockSpec((1, GROUP), lambda i: (0, 0))
    return pl.pallas_call(
        _moba_prep_kernel,
        grid=(nt,),
        in_specs=[pl.BlockSpec((MOBA_BLOCK, GW), lambda i: (i, 3)), tspec, tspec, cspec, cspec,
                  pl.BlockSpec((GROUP, GROUP), lambda i: (0, 0))],
        out_specs=[rspec, rspec, rspec, pl.BlockSpec((GROUP, MOBA_BLOCK), lambda i: (0, i)),
                   pl.BlockSpec((None, 1, GROUP), lambda i: (i, 0, 0))],
        out_shape=[jax.ShapeDtypeStruct((rows, GROUP), F32), jax.ShapeDtypeStruct((rows, GROUP), F32),
                   jax.ShapeDtypeStruct((rows, GROUP), BF16), jax.ShapeDtypeStruct((GROUP, rows), BF16),
                   jax.ShapeDtypeStruct((nt, 1, GROUP), F32)],
        compiler_params=_cparams(("parallel",)),
        name="moba_prep",
    )(p, cos_t, sin_t, jnp.tile(q_norm, A_HEADS).reshape(1, GROUP), jnp.tile(k_norm, A_HEADS).reshape(1, GROUP),
      _head_block_ones(GROUP, A_HEAD))


def _moba_prompt_kernel(q_ref, kb_ref, vt_ref, km_ref, y_ref, *, nb):
    i = pl.program_id(1)
    blk = MOBA_BLOCK
    scale = A_HEAD ** -0.5
    lane = _iota((blk, LANES), 1)
    lane_km = _iota((nb, LANES), 1)
    bidx = _iota((nb, blk), 0)
    causal = _iota((blk, blk), 0) <= _iota((blk, blk), 1)
    row0 = pl.multiple_of(i * blk, blk)
    prs = range(A_HEADS // 2)
    k_own = [kb_ref[pl.ds(row0, blk), 128 * pr:128 * pr + 128] for pr in prs]
    vt_own = [vt_ref[128 * pr:128 * pr + 128, pl.ds(row0, blk)] for pr in prs]
    hp = [(pr, half) for pr in prs for half in range(2)]
    qbs = [(jnp.where((lane // 64) == half, q_ref[:, 128 * pr:128 * pr + 128], 0.0) * scale).astype(BF16)
           for pr, half in hp]
    gates = [_dot_nt(jnp.where((lane_km // 64) == half, km_ref[:, 128 * pr:128 * pr + 128], 0.0),
                     q_ref[:, 128 * pr:128 * pr + 128], HI) for pr, half in hp]
    s_owns = [_dot_nt(k_own[pr], qb) for (pr, _), qb in zip(hp, qbs)]
    heads, p0s = [], []
    for (pr, _), qb, gate, s_own in zip(hp, qbs, gates, s_owns):
        gate = jnp.where(bidx < i, gate, -jnp.inf)
        picks = []
        for _ in range(MOBA_TOPK):
            mx = jnp.max(gate, axis=0, keepdims=True)
            idx = jnp.min(jnp.where(gate == mx, bidx, nb), axis=0, keepdims=True)
            picks.append(jnp.where(mx > -jnp.inf, idx, -1))
            gate = jnp.where(bidx == idx, -jnp.inf, gate)
        s_own = jnp.where(causal, s_own, NEG)
        m0 = jnp.max(s_own, axis=0, keepdims=True)
        p0 = jnp.exp(s_own - m0)
        p0s.append(p0.astype(BF16))
        heads.append(dict(pair=pr, qb=qb, picks=picks, m0=m0, l0=jnp.sum(p0, axis=0, keepdims=True)))
    acc0s = [_dot(vt_own[hd['pair']], p0) for hd, p0 in zip(heads, p0s)]
    inits = tuple((hd['m0'], hd['l0'], acc0) for hd, acc0 in zip(heads, acc0s))

    def body(j, carry):
        c0 = pl.multiple_of(j * blk, blk)
        kjs = [kb_ref[pl.ds(c0, blk), 128 * pr:128 * pr + 128] for pr in range(A_HEADS // 2)]
        vtjs = [vt_ref[128 * pr:128 * pr + 128, pl.ds(c0, blk)] for pr in range(A_HEADS // 2)]
        scores = [_dot_nt(kjs[hd['pair']], hd['qb']) for hd in heads]
        sts = []
        for hd, (m, l, acc), sc in zip(heads, carry, scores):
            picks = hd['picks']
            sel = jnp.logical_or(jnp.logical_or(picks[0] == j, picks[1] == j), picks[2] == j)
            sj = jnp.where(sel, sc, NEG)
            m_new = jnp.maximum(m, jnp.max(sj, axis=0, keepdims=True))
            alpha = jnp.exp(m - m_new)
            pj = jnp.exp(sj - m_new)
            sts.append((m_new, alpha, alpha * l + jnp.sum(pj, axis=0, keepdims=True), pj.astype(BF16)))
        pvs = [_dot(vtjs[hd['pair']], st[3]) for hd, st in zip(heads, sts)]
        return tuple((st[0], st[2], st[1] * acc + pv) for st, (_, _, acc), pv in zip(sts, carry, pvs))

    final = lax.fori_loop(0, i, body, inits)
    outs = [acc / l for _, l, acc in final]
    for pair in range(A_HEADS // 2):
        both = jnp.concatenate([outs[2 * pair][0:64], outs[2 * pair + 1][64:128]], axis=0)
        y_ref[:, 128 * pair:128 * pair + 128] = both.T


def _moba_prompt(q, kb, vt, km, *, nbatch, t):
    nb = t // MOBA_BLOCK
    return pl.pallas_call(
        functools.partial(_moba_prompt_kernel, nb=nb),
        grid=(nbatch, nb),
        in_specs=[pl.BlockSpec((MOBA_BLOCK, GROUP), lambda b, i: (b * nb + i, 0)),
                  pl.BlockSpec((t, GROUP), lambda b, i: (b, 0)),
                  pl.BlockSpec((GROUP, t), lambda b, i: (0, b)),
                  pl.BlockSpec((None, nb, GROUP), lambda b, i: (b, 0, 0))],
        out_specs=pl.BlockSpec((MOBA_BLOCK, GROUP), lambda b, i: (b * nb + i, 0)),
        out_shape=jax.ShapeDtypeStruct((nbatch * t, GROUP), F32),
        compiler_params=_cparams(("parallel", "arbitrary")),
        name="moba_prompt",
    )(q, kb, vt, km)


def _moba_gate_kernel(pt_ref, *refs, npages_step, nblocks, page):
    del pt_ref
    pg_refs = refs[:npages_step]
    qb_ref, sel_ref, g_sc = refs[npages_step:]
    st = pl.program_id(1)
    per = npages_step // 2
    tq = qb_ref.shape[0]
    for i in range(per):
        part = pg_refs[2 * i][...] + pg_refs[2 * i + 1][...]
        for qi in range(tq):
            prod = part * qb_ref[qi]
            g_sc[qi, st * per + i] = jnp.concatenate(
                [jnp.sum(prod[h], axis=0, keepdims=True) for h in range(A_HEADS)], axis=0)

    @pl.when(st == pl.num_programs(1) - 1)
    def _():
        nidx = _iota((nblocks, A_HEADS, 1), 0)
        for qi in range(tq):
            gate = jnp.sum(g_sc[qi], axis=-1, keepdims=True) * (1.0 / (2 * page))
            for slot in range(MOBA_TOPK):
                mx = jnp.max(gate, axis=0, keepdims=True)
                idx = jnp.min(jnp.where(gate == mx, nidx, nblocks), axis=0, keepdims=True)
                sel_ref[qi * MOBA_TOPK + slot] = jnp.broadcast_to(idx[0], (A_HEADS, LANES))
                gate = jnp.where(nidx == idx, -jnp.inf, gate)


def _moba_gate(cache_kt, layer, page_table, qb):
    nseq, npages = page_table.shape
    page = cache_kt.shape[-1]
    nblocks = npages * page // MOBA_BLOCK
    npages_step = 8
    tq = qb.shape[1]

    def pg_spec(i):
        return pl.BlockSpec((None, None, A_HEADS, A_HEAD, page),
                            lambda b, st, pt: (layer, pt[b, st * npages_step + i], 0, 0, 0))

    return pl.pallas_call(
        functools.partial(_moba_gate_kernel, npages_step=npages_step, nblocks=nblocks, page=page),
        grid_spec=pltpu.PrefetchScalarGridSpec(
            num_scalar_prefetch=1,
            grid=(nseq, npages // npages_step),
            in_specs=[pg_spec(i) for i in range(npages_step)]
            + [pl.BlockSpec((None, tq, A_HEADS, A_HEAD, page), lambda b, st, pt: (b, 0, 0, 0, 0))],
            out_specs=pl.BlockSpec((None, tq * MOBA_TOPK, A_HEADS, LANES), lambda b, st, pt: (b, 0, 0, 0)),
            scratch_shapes=[pltpu.VMEM((tq, nblocks, A_HEADS, page), F32)]),
        out_shape=jax.ShapeDtypeStruct((nseq, tq * MOBA_TOPK, A_HEADS, LANES), jnp.int32),
        compiler_params=_cparams(("parallel", "arbitrary")),
        name="moba_gate",
    )(page_table, *([cache_kt] * npages_step), qb)


def _moba_sample_kernel(pg_ref, q_ref, kn_ref, vn_ref, ck_ref, cv_ref, o_ref, kbuf, vbuf, sem, *, layer, tq, page):
    b, h = pl.program_id(0), pl.program_id(1)
    nslab = tq * MOBA_TOPK * (MOBA_BLOCK // page)
    base = (b * pl.num_programs(1) + h) * nslab

    def copies(c):
        pg = pg_ref[base + c]
        return (pltpu.make_async_copy(ck_ref.at[layer, pg, h], kbuf.at[:, pl.ds(c * page, page)], sem.at[0]),
                pltpu.make_async_copy(cv_ref.at[layer, pg, h], vbuf.at[:, pl.ds(c * page, page)], sem.at[1]))

    for c in range(nslab):
        for cp in copies(c):
            cp.start()
    for c in range(nslab):
        for cp in copies(c):
            cp.wait()
    scale = A_HEAD ** -0.5
    nk = nslab * page
    q = q_ref[...]
    rows = q.shape[0]
    owner = _iota((rows, nk), 1) // (MOBA_TOPK * MOBA_BLOCK)
    s_sel = jnp.where(owner == _iota((rows, nk), 0), _dotb(q, kbuf[...]) * scale, NEG)
    r8, c8 = _iota((rows, rows), 0), _iota((rows, rows), 1)
    s_own = jnp.where(jnp.logical_and(c8 <= r8, c8 < tq), _dotb_nt(q, kn_ref[...]) * scale, NEG)
    m = jnp.maximum(jnp.max(s_sel, axis=1, keepdims=True), jnp.max(s_own, axis=1, keepdims=True))
    p_sel = jnp.exp(s_sel - m)
    p_own = jnp.exp(s_own - m)
    l = jnp.sum(p_sel, axis=1, keepdims=True) + jnp.sum(p_own, axis=1, keepdims=True)
    o_ref[...] = (_dotb_nt(p_sel, vbuf[...]) + _dotb(p_own, vn_ref[...])) / l


def _moba_sample(pages, q_s, kn_s, vn_s, cache_kt, cache_vt, layer, *, tq):
    nseq, nh, rows, hd = q_s.shape
    page = cache_kt.shape[-1]
    nslab = tq * MOBA_TOPK * (MOBA_BLOCK // page)
    spec = pl.BlockSpec((None, None, rows, hd), lambda b, h, pg: (b, h, 0, 0))
    return pl.pallas_call(
        functools.partial(_moba_sample_kernel, layer=layer, tq=tq, page=page),
        grid_spec=pltpu.PrefetchScalarGridSpec(
            num_scalar_prefetch=1,
            grid=(nseq, nh),
            in_specs=[spec, spec, spec, pl.BlockSpec(memory_space=pl.ANY), pl.BlockSpec(memory_space=pl.ANY)],
            out_specs=spec,
            scratch_shapes=[pltpu.VMEM((hd, nslab * page), F32), pltpu.VMEM((hd, nslab * page), F32),
                            pltpu.SemaphoreType.DMA((2,))]),
        out_shape=jax.ShapeDtypeStruct((nseq, nh, rows, hd), F32),
        compiler_params=_cparams(("arbitrary", "arbitrary")),
        name="moba_sample",
    )(pages, q_s, kn_s, vn_s, cache_kt, cache_vt)


def _pack_w_in(w):
    def dz(n):
        return jnp.zeros(w.shape[:2] + (n,), w.dtype)

    o_r, o_g, o_a = P_M, P_M + P_R, P_M + P_R + P_G
    kd = 2 * G_HEADS * G_DK + GROUP
    parts = [w[..., 0:P_M], dz(GW - P_M),
             w[..., o_r:o_r + P_R], dz(GW - P_R),
             w[..., o_g:o_g + kd], w[..., o_g + kd + G_LORA:o_g + P_G], w[..., o_g + kd:o_g + kd + G_LORA],
             dz(GW - P_G),
             w[..., o_a:o_a + P_A], dz(GW - P_A)]
    return jnp.concatenate(parts, axis=-1).astype(BF16)


def _even_head(nheads):
    return (jnp.arange(nheads) % 2 == 0).reshape((nheads, 1, 1))


def _place_half(x, nheads, axis):
    z = jnp.zeros_like(x)
    even = _even_head(nheads)
    return jnp.concatenate([jnp.where(even, x, z), jnp.where(even, z, x)], axis=axis)


def _take_half(x, nheads, axis):
    even = _even_head(nheads)
    lo = lax.slice_in_dim(x, 0, 64, axis=x.ndim + axis)
    hi = lax.slice_in_dim(x, 64, 128, axis=x.ndim + axis)
    return jnp.where(even, lo, hi)


def kernel(x_prompt, x_sample, cache_k, cache_v, state_mlstm_c, state_mlstm_n, state_mlstm_m, state_rwkv, state_rwkv_shift, state_gla, page_table, ffn1_norm, ffn1_w_in, ffn1_w_out, mix_norm, w_in, w_out, ffn2_norm, ffn2_w_in, ffn2_w_out, mlstm_gate_b, mlstm_norm_w, rwkv_mu, rwkv_w0, rwkv_w2, rwkv_a0, rwkv_a2, rwkv_g2, rwkv_k_k, rwkv_k_a, rwkv_r_k, rwkv_ln_w, rwkv_ln_b, gla_gk_up, gla_gk_b, gla_norm_w, moba_q_norm, moba_k_norm):
    nb, t, d = x_prompt.shape
    ns, ts, _ = x_sample.shape
    depth = w_in.shape[0]
    assert nb == 2 and ns % 2 == 0 and t % MOBA_BLOCK == 0 and ts <= 8
    npc = t // CH
    n_prompt = nb * t
    rows = n_prompt + ns * CH
    page = cache_k.shape[2]
    past_len = page_table.shape[1] * page
    cache_kt = jnp.transpose(cache_k, (0, 1, 3, 4, 2))
    cache_vt = jnp.transpose(cache_v, (0, 1, 3, 4, 2))
    kw = dict(npc=npc, n_sample=ns, t_s=ts)
    tm = 512 if rows % 512 == 0 else 256

    x = jnp.concatenate([x_prompt.reshape(n_prompt, d),
                         jnp.pad(x_sample, ((0, 0), (0, CH - ts), (0, 0))).reshape(ns * CH, d)], axis=0)
    pos = jnp.concatenate([jnp.tile(jnp.arange(t), nb), jnp.tile(past_len + jnp.arange(CH), ns)])
    inv = ROPE_THETA ** (-(jnp.arange(A_ROT // 2, dtype=F32) * 2.0 / A_ROT))
    ang = pos.astype(F32)[:, None] * inv[None, :]
    one = jnp.ones((rows, A_HEAD - A_ROT), F32)
    cos_h = jnp.concatenate([jnp.cos(ang), jnp.cos(ang), one], axis=1)
    sin_h = jnp.concatenate([-jnp.sin(ang), jnp.sin(ang), 0.0 * one], axis=1)
    cos_t = jnp.concatenate([cos_h, cos_h], axis=1)
    sin_t = jnp.concatenate([sin_h, sin_h], axis=1)

    w_in_p = _pack_w_in(w_in)
    w_out_b = w_out.astype(BF16)
    f1_in, f1_out = ffn1_w_in.astype(BF16), ffn1_w_out.astype(BF16)
    f2_in, f2_out = ffn2_w_in.astype(BF16), ffn2_w_out.astype(BF16)

    def with_prompt_zeros(st):
        return jnp.concatenate([jnp.zeros((nb,) + st.shape[1:], F32), st], axis=0)

    def sample_rows(a, n):
        return a[n_prompt:].reshape(ns, CH, a.shape[-1])[:, :n]

    def heads_major(a):
        return a.reshape(ns, 8, A_HEADS, A_HEAD).transpose(0, 2, 1, 3)

    per_layer = []
    for l in range(depth):
        x = _ffn(x, ffn1_norm[l], f1_in[l], f1_out[l], tm=tm)
        p = _proj(x, mix_norm[l], w_in_p[l], tm=tm)
        p3 = p.reshape(rows // CH, CH, NPK)

        ce = jnp.concatenate([state_mlstm_c[l], state_mlstm_n[l][..., None],
                              jnp.zeros(state_mlstm_n[l].shape + (LANES - 1,), F32)], axis=-1)
        ce0 = with_prompt_zeros(_place_half(ce, M_HEADS, -2))
        m0 = jnp.zeros((nb + ns, 8, LANES), F32).at[nb:, 0, :M_HEADS].set(state_mlstm_m[l])
        ym_p, ym_s, ce_a, ce_b, m_a, m_b = _mlstm(p3, ce0, m0, mlstm_gate_b[l], mlstm_norm_w[l], **kw)
        ce_p, ce_s = (_take_half(c, M_HEADS, -2) for c in _merge_state(ce_a, ce_b))
        mm_p, mm_s = (m[:, 0, :M_HEADS] for m in _merge_state(m_a, m_b))

        st_t = jnp.swapaxes(state_rwkv[l], -1, -2)
        w0s = with_prompt_zeros(_place_half(_place_half(st_t, R_HEADS, -2), R_HEADS, -1))
        sh0 = jnp.zeros((nb + ns, 8, GW), F32).at[nb:, :, :P_R].set(state_rwkv_shift[l][:, None, :])
        lw = dict(rwkv_mu=rwkv_mu[l], rwkv_w0=rwkv_w0[l], rwkv_w2=rwkv_w2[l], rwkv_a0=rwkv_a0[l],
                  rwkv_a2=rwkv_a2[l], rwkv_g2=rwkv_g2[l], rwkv_k_k=rwkv_k_k[l], rwkv_k_a=rwkv_k_a[l],
                  rwkv_r_k=rwkv_r_k[l].reshape(-1), rwkv_ln_w=rwkv_ln_w[l], rwkv_ln_b=rwkv_ln_b[l])
        yr_p, yr_s, w_a, w_b, sh_a, sh_b = _rwkv(p3, w0s, sh0, lw, **kw)
        rw_p, rw_s = (jnp.swapaxes(_take_half(_take_half(w, R_HEADS, -2), R_HEADS, -1), -1, -2)
                      for w in _merge_state(w_a, w_b))
        sh_p, sh_s = (sh[:, 0, :P_R] for sh in _merge_state(sh_a, sh_b))

        s0 = with_prompt_zeros(_place_half(state_gla[l], G_HEADS, -2))
        yg_p, yg_s, g_a, g_b = _gla(p3, s0, gla_gk_up[l], gla_gk_b[l], gla_norm_w[l], **kw)
        gl_p, gl_s = (_take_half(gs, G_HEADS, -2) for gs in _merge_state(g_a, g_b))

        q_all, k_all, kb, vt, km = _moba_prep(p, cos_t, sin_t, moba_q_norm[l], moba_k_norm[l])
        v_all = p[:, 3 * GW + 2 * GROUP:3 * GW + 3 * GROUP]
        nbk = t // MOBA_BLOCK
        ya_p = _moba_prompt(q_all, kb, vt, km[:nb * nbk].reshape(nb, nbk, GROUP), nbatch=nb, t=t)
        q_s = sample_rows(q_all, 8)
        q_lanes = jnp.broadcast_to(q_s[:, :ts].reshape(ns, ts, A_HEADS, A_HEAD, 1), (ns, ts, A_HEADS, A_HEAD, page))
        sel = _moba_gate(cache_kt, l, page_table, q_lanes)[..., 0]
        sel = sel.reshape(ns, ts, MOBA_TOPK, A_HEADS).transpose(0, 3, 1, 2)
        per_blk = MOBA_BLOCK // page
        pidx = per_blk * sel[..., None] + jnp.arange(per_blk)
        pages = page_table[jnp.arange(ns).reshape(ns, 1, 1, 1, 1), pidx].reshape(-1).astype(jnp.int32)
        o_s = _moba_sample(pages, heads_major(q_s), heads_major(sample_rows(k_all, 8)),
                           heads_major(sample_rows(v_all, 8)), cache_kt, cache_vt, l, tq=ts)
        ya_s = jnp.pad(o_s.transpose(0, 2, 1, 3).reshape(ns, 8, GROUP), ((0, 0), (0, CH - 8), (0, 0)))
        ya = jnp.concatenate([ya_p, ya_s.reshape(ns * CH, GROUP)], axis=0)

        x = _outproj(x, _merge_y(ym_p, ym_s), _merge_y(yr_p, yr_s), _merge_y(yg_p, yg_s), ya, w_out_b[l], tm=tm)
        x = _ffn(x, ffn2_norm[l], f2_in[l], f2_out[l], tm=tm)

        shp_p, shp_s = (nb, t, A_HEADS, A_HEAD), (ns, ts, A_HEADS, A_HEAD)
        per_layer.append((
            k_all[:n_prompt].reshape(shp_p), v_all[:n_prompt].reshape(shp_p),
            sample_rows(k_all, ts).reshape(shp_s), sample_rows(v_all, ts).reshape(shp_s),
            ce_p[..., :M_DV], ce_s[..., :M_DV], ce_p[..., M_DV], ce_s[..., M_DV], mm_p, mm_s,
            rw_p, rw_s, sh_p, sh_s, gl_p, gl_s))

    y_p = x[:n_prompt].reshape(nb, t, d)
    y_s = x[n_prompt:].reshape(ns, CH, d)[:, :ts]
    stacked = tuple(jnp.stack([lay[i] for lay in per_layer]) for i in range(len(per_layer[0])))
    return (y_p, y_s) + stacked
```

```python
import functools

import jax
import jax.numpy as jnp
import numpy as np
from jax import lax
from jax.experimental import pallas as pl
from jax.experimental.pallas import tpu as pltpu

F32 = jnp.float32
BF16 = jnp.bfloat16
HI = lax.Precision.HIGHEST

D_MODEL = 2048
GROUP = D_MODEL // 4
M_HEADS = 4
M_DV = GROUP // M_HEADS
M_DK = M_DV // 2
GATE_CAP = 15.0
R_HEAD = 64
R_HEADS = GROUP // R_HEAD
R_LORA = 32
R_GN_EPS = 64e-5
G_HEADS = 4
G_DV = GROUP // G_HEADS
G_DK = G_DV // 2
G_LORA = 16
G_TAU = 16.0
A_HEAD = 64
A_HEADS = GROUP // A_HEAD
A_ROT = A_HEAD // 4
ROPE_THETA = 500000.0
MOBA_BLOCK = 256
MOBA_TOPK = 3
RMS_EPS = 1e-6
P_M = 2 * M_HEADS * M_DK + 2 * GROUP + 2 * M_HEADS
P_R = 3 * GROUP + 3 * R_LORA
P_G = 2 * G_HEADS * G_DK + 2 * GROUP + G_LORA
P_A = 3 * GROUP

CH = 64
GW = 1664
NPK = 4 * GW
NEG = -1e30
LANES = 128
VMEM_LIMIT = 48 * 1024 * 1024


def _dot(a, b, prec=None):
    return jnp.dot(a, b, preferred_element_type=F32, precision=prec)


def _dot_nt(a, b, prec=None):
    return lax.dot_general(a, b, (((1,), (1,)), ((), ())), preferred_element_type=F32, precision=prec)


def _dot_tn(a, b, prec=None):
    return lax.dot_general(a, b, (((0,), (0,)), ((), ())), preferred_element_type=F32, precision=prec)


def _bf(x):
    return x.astype(BF16)


def _dotb(a, b):
    return _dot(_bf(a), _bf(b))


def _dotb_nt(a, b):
    return _dot_nt(_bf(a), _bf(b))


def _dotb_tn(a, b):
    return _dot_tn(_bf(a), _bf(b))


def _split2(x):
    hi = x.astype(BF16)
    return hi, (x - hi.astype(F32)).astype(BF16)


def _dot_sel_rhs(a, sel):
    hi, lo = _split2(a)
    sel = _bf(sel)
    return _dot(hi, sel) + _dot(lo, sel)


def _dot_sel_lhs(sel, b):
    hi, lo = _split2(b)
    sel = _bf(sel)
    return _dot(sel, hi) + _dot(sel, lo)


def _row_to_col(row, eye):
    return jnp.sum(eye * row, axis=1, keepdims=True)


def _iota(shape, dim):
    return lax.broadcasted_iota(jnp.int32, shape, dim)


def _sigmoid(x):
    return 1.0 / (1.0 + jnp.exp(-x))


def _softplus(x):
    return jnp.maximum(x, 0.0) + jnp.log(1.0 + jnp.exp(-jnp.abs(x)))


def _log_sigmoid(x):
    return -_softplus(-x)


def _tril(n):
    return (_iota((n, n), 1) <= _iota((n, n), 0)).astype(F32)


def _cparams(sem, vmem=VMEM_LIMIT):
    return pltpu.CompilerParams(dimension_semantics=sem, vmem_limit_bytes=vmem)


def _rms_rows(x, w):
    return x * lax.rsqrt(jnp.mean(x * x, axis=-1, keepdims=True) + RMS_EPS) * w


def _sample_rows(ref, rows_per_slot):
    tm, c = ref.shape
    return ref[...].reshape(tm // CH, CH, c)[:, 0:rows_per_slot, :].reshape(tm // CH * rows_per_slot, c)


def _scatter_sample_rows(o_ref, vals, rows_per_slot):
    for b in range(o_ref.shape[0] // CH):
        o_ref[CH * b:CH * b + rows_per_slot, :] = vals[rows_per_slot * b:rows_per_slot * (b + 1), :]


def _ffn_kernel(x_ref, nw_ref, wg_ref, wu_ref, wo_ref, o_ref, h_ref, *, prompt_tiles, ms):
    i, j = pl.program_id(0), pl.program_id(1)
    nj = pl.num_programs(1)

    def run(rows, x_rows):
        @pl.when(j == 0)
        def _():
            h_ref[0:rows, :] = _rms_rows(x_rows(), nw_ref[...]).astype(BF16)

        h = h_ref[0:rows, :]
        gate = _dot(h, wg_ref[...])
        up = _dot(h, wu_ref[...])
        return _dot((gate * _sigmoid(gate) * up).astype(BF16), wo_ref[...])

    @pl.when(i < prompt_tiles)
    def _():
        part = run(o_ref.shape[0], lambda: x_ref[...])

        @pl.when(j == 0)
        def _():
            o_ref[...] = part

        @pl.when(j > 0)
        def _():
            o_ref[...] += part

        @pl.when(j == nj - 1)
        def _():
            o_ref[...] = x_ref[...] + 0.5 * o_ref[...]

    @pl.when(i >= prompt_tiles)
    def _():
        part = run(ms, lambda: _sample_rows(x_ref, 8))

        @pl.when(j == 0)
        def _():
            o_ref[...] = x_ref[...]
            acc_rows = part
            _scatter_sample_rows(o_ref, acc_rows, 8)

        @pl.when(j > 0)
        def _():
            _scatter_sample_rows(o_ref, _sample_rows(o_ref, 8) + part, 8)

        @pl.when(j == nj - 1)
        def _():
            _scatter_sample_rows(o_ref, _sample_rows(x_ref, 8) + 0.5 * _sample_rows(o_ref, 8), 8)


def _ffn(x, norm_w, w_in, w_out, *, tm, n_prompt, tf=512):
    rows, d = x.shape
    dff = w_out.shape[0]
    nj = dff // tf
    return pl.pallas_call(
        functools.partial(_ffn_kernel, prompt_tiles=n_prompt // tm, ms=tm // CH * 8),
        grid=(rows // tm, nj),
        in_specs=[
            pl.BlockSpec((tm, d), lambda i, j: (i, 0)),
            pl.BlockSpec((1, d), lambda i, j: (0, 0)),
            pl.BlockSpec((d, tf), lambda i, j: (0, j)),
            pl.BlockSpec((d, tf), lambda i, j: (0, j + nj)),
            pl.BlockSpec((tf, d), lambda i, j: (j, 0)),
        ],
        out_specs=pl.BlockSpec((tm, d), lambda i, j: (i, 0)),
        out_shape=jax.ShapeDtypeStruct((rows, d), F32),
        scratch_shapes=[pltpu.VMEM((tm, d), BF16)],
        compiler_params=_cparams(("parallel", "arbitrary")),
        name="ffn",
    )(x, norm_w.reshape(1, d), w_in, w_in, w_out)


def _proj_kernel(x_ref, nw_ref, w_ref, o_ref, h_ref, *, prompt_tiles, ms):
    i, j = pl.program_id(0), pl.program_id(1)

    @pl.when(i < prompt_tiles)
    def _():
        @pl.when(j == 0)
        def _():
            h_ref[...] = _rms_rows(x_ref[...], nw_ref[...]).astype(BF16)

        o_ref[...] = _dot(h_ref[...], w_ref[...])

    @pl.when(i >= prompt_tiles)
    def _():
        @pl.when(j == 0)
        def _():
            h_ref[0:ms, :] = _rms_rows(_sample_rows(x_ref, 8), nw_ref[...]).astype(BF16)

        o_ref[...] = jnp.zeros_like(o_ref)
        _scatter_sample_rows(o_ref, _dot(h_ref[0:ms, :], w_ref[...]), 8)


def _proj(x, norm_w, w, *, tm, n_prompt, tn=GW):
    rows, d = x.shape
    n = w.shape[1]
    return pl.pallas_call(
        functools.partial(_proj_kernel, prompt_tiles=n_prompt // tm, ms=tm // CH * 8),
        grid=(rows // tm, n // tn),
        in_specs=[
            pl.BlockSpec((tm, d), lambda i, j: (i, 0)),
            pl.BlockSpec((1, d), lambda i, j: (0, 0)),
            pl.BlockSpec((d, tn), lambda i, j: (0, j)),
        ],
        out_specs=pl.BlockSpec((tm, tn), lambda i, j: (i, j)),
        out_shape=jax.ShapeDtypeStruct((rows, n), F32),
        scratch_shapes=[pltpu.VMEM((tm, d), BF16)],
        compiler_params=_cparams(("parallel", "arbitrary")),
        name="proj",
    )(x, norm_w.reshape(1, d), w)


def _outproj_kernel(x_ref, *refs, prompt_tiles):
    yp_refs, ys_refs, w_ref, o_ref = refs[0:4], refs[4:8], refs[8], refs[9]

    def run(y_refs):
        acc = x_ref[...]
        for gi, y_ref in enumerate(y_refs):
            acc = acc + _dot(y_ref[...].astype(BF16), w_ref[gi * GROUP:(gi + 1) * GROUP, :])
        o_ref[...] = acc

    @pl.when(pl.program_id(0) < prompt_tiles)
    def _():
        run(yp_refs)

    @pl.when(pl.program_id(0) >= prompt_tiles)
    def _():
        run(ys_refs)


def _outproj(x, y_prompt, y_sample, w, *, tm):
    rows, d = x.shape
    pt = y_prompt[0].shape[0] // tm
    p_spec = pl.BlockSpec((tm, GROUP), lambda i: (jnp.minimum(i, pt - 1), 0))
    s_spec = pl.BlockSpec((tm, GROUP), lambda i: (jnp.maximum(i - pt, 0), 0))
    return pl.pallas_call(
        functools.partial(_outproj_kernel, prompt_tiles=pt),
        grid=(rows // tm,),
        in_specs=[pl.BlockSpec((tm, d), lambda i: (i, 0))] + [p_spec] * 4 + [s_spec] * 4
        + [pl.BlockSpec((d, d), lambda i: (0, 0))],
        out_specs=pl.BlockSpec((tm, d), lambda i: (i, 0)),
        out_shape=jax.ShapeDtypeStruct((rows, d), F32),
        compiler_params=_cparams(("parallel",)),
        name="outproj",
    )(x, *y_prompt, *y_sample, w)


def _layer_spec(shape, layer):
    return pl.BlockSpec((None,) + shape, lambda g: (layer,) + (0,) * len(shape))


def _mixer_specs(npc, group, state_blocks, layer):
    def chunk_idx(s):
        return lambda g: (jnp.where(g < npc, s * npc + g, 2 * npc + 2 * (g - npc) + s), 0, group)

    p_specs = [pl.BlockSpec((None, CH, GW), chunk_idx(s)) for s in (0, 1)]
    in_state, out_state = [], []
    for blk in state_blocks:
        zeros = (0,) * len(blk)
        for s in (0, 1):
            in_state.append(pl.BlockSpec(
                (None, None) + blk,
                lambda g, s=s, z=zeros: (layer, jnp.where(g < npc, s, 2 + 2 * (g - npc) + s)) + z))
            out_state.append(pl.BlockSpec(
                (None,) + blk, lambda g, z=zeros: (jnp.where(g < npc, 0, 1 + g - npc),) + z))
    y_specs = [pl.BlockSpec((2, None, CH, GROUP), lambda g: (0, jnp.minimum(g, npc - 1), 0, 0)),
               pl.BlockSpec((None, 2, CH, GROUP), lambda g: (jnp.maximum(g - npc, 0), 0, 0, 0))]
    return p_specs, in_state, out_state, y_specs


def _mixer_out_shapes(npc, n_sample, state_blocks):
    ys = [jax.ShapeDtypeStruct((2, npc, CH, GROUP), F32), jax.ShapeDtypeStruct((n_sample // 2, 2, CH, GROUP), F32)]
    st = []
    for blk in state_blocks:
        st += [jax.ShapeDtypeStruct((1 + n_sample // 2,) + blk, F32)] * 2
    return ys, st


def _step_info(npc, t_s):
    g = pl.program_id(0)
    first = jnp.logical_or(g == 0, g >= npc)
    last = g >= npc - 1
    tlen = jnp.where(g < npc, CH, t_s)
    return g, first, last, tlen


def _write_y(g, npc, yp_ref, ys_ref, s, val):
    @pl.when(g < npc)
    def _():
        yp_ref[s] = val

    @pl.when(g >= npc)
    def _():
        ys_ref[s] = val


def _merge_state(a, b, nb):
    assert nb == 2
    prompt = jnp.stack([a[:, 0], b[:, 0]], axis=1)
    sample = jnp.stack([a[:, 1:], b[:, 1:]], axis=2).reshape((a.shape[0], -1) + a.shape[2:])
    return prompt, sample


def _mlstm_kernel(pa_ref, pb_ref, cea_ref, ceb_ref, ma_ref, mb_ref, gb_ref, nw_ref,
                  yp_ref, ys_ref, ceoa_ref, ceob_ref, moa_ref, mob_ref, ce_sc, m_sc, *, npc, t_s):
    g, first, last, tlen = _step_info(npc, t_s)
    slots = ((pa_ref, cea_ref, ma_ref, ceoa_ref, moa_ref), (pb_ref, ceb_ref, mb_ref, ceob_ref, mob_ref))
    row = _iota((CH, LANES), 0)
    lane = _iota((CH, LANES), 1)
    tril = _tril(CH)
    causal = _iota((CH, CH), 1) <= _iota((CH, CH), 0)
    e0 = (lane == 0).astype(F32)
    valid = row < tlen

    @pl.when(first)
    def _():
        for s, (_, ce0_ref, m0_ref, _, _) in enumerate(slots):
            ce_sc[s] = ce0_ref[...]
            m_sc[s] = m0_ref[...]

    units = []
    for s, (p_ref, _, _, _, _) in enumerate(slots):
        gates = p_ref[:, 1536:1664] + gb_ref[...]
        gates = GATE_CAP * jnp.tanh(gates / GATE_CAP)
        ig = jnp.where(valid, gates, NEG)
        lf = jnp.where(valid, _log_sigmoid(gates), 0.0)
        b_col = _dot_sel_lhs(tril, lf)
        ig_t = ig.T
        lf_hi, lf_lo = _split2(lf.T[0:8])
        b_row = _dot_nt(lf_hi, _bf(tril)) + _dot_nt(lf_lo, _bf(tril))
        for h in range(M_HEADS):
            pair, half = h // 2, h % 2
            hm = (lane // 64) == half
            b_c = b_col[:, 4 + h:5 + h]
            m_h = m_sc[s, 0:1, h:h + 1]
            dmat = jnp.where(causal, b_c - b_row[4 + h:5 + h, :] + ig_t[h:h + 1, :], NEG)
            m_inter = b_c + m_h
            mt = jnp.maximum(m_inter, jnp.max(dmat, axis=1, keepdims=True))
            m_new = mt[CH - 1:CH, :]
            b_last = b_c[CH - 1:CH, :]
            units.append(dict(
                s=s, h=h, p_ref=p_ref, mt=mt, m_new=m_new, dexp=jnp.exp(dmat - mt), s_inter=jnp.exp(m_inter - mt),
                carry=jnp.exp(b_last + m_h - m_new), ws=jnp.exp(b_last - b_c + ig[:, h:h + 1] - m_new),
                qm=jnp.where(hm, p_ref[:, 128 * pair:128 * pair + 128], 0.0),
                km=jnp.where(hm, p_ref[:, 256 + 128 * pair:256 + 128 * pair + 128], 0.0) * (M_DK ** -0.5),
                v_ext=jnp.concatenate([p_ref[:, 512 + 128 * h:512 + 128 * h + 128], e0], axis=1)))
    ces = [ce_sc[u['s'], u['h']] for u in units]
    qks = [_dotb_nt(u['qm'], u['km']) for u in units]
    qcs = [_dotb(u['qm'], ce) for u, ce in zip(units, ces)]
    nds = [u['s_inter'] * qc + _dotb(u['dexp'] * qk, u['v_ext']) for u, qk, qc in zip(units, qks, qcs)]
    for u, ce in zip(units, ces):
        ce_sc[u['s'], u['h']] = u['carry'] * ce + _dotb_tn(u['km'], u['ws'] * u['v_ext'])
        m_sc[u['s'], 0:1, u['h']:u['h'] + 1] = u['m_new']
    for s, (p_ref, _, _, ceo_ref, mo_ref) in enumerate(slots):
        ys = []
        for h in range(M_HEADS):
            u, nd = units[s * M_HEADS + h], nds[s * M_HEADS + h]
            hh = nd[:, 0:128] / jnp.maximum(jnp.abs(nd[:, 128:129]), jnp.exp(-u['mt']))
            hn = hh * lax.rsqrt(jnp.mean(hh * hh, axis=-1, keepdims=True) + RMS_EPS)
            hn = hn * nw_ref[:, 128 * h:128 * h + 128]
            ys.append(hn * _sigmoid(p_ref[:, 1024 + 128 * h:1024 + 128 * h + 128]))
        _write_y(g, npc, yp_ref, ys_ref, s, jnp.concatenate(ys, axis=1))

        @pl.when(last)
        def _():
            ceo_ref[...] = ce_sc[s]
            mo_ref[...] = m_sc[s]


def _mlstm(p3, ce0, m0, gate_b, norm_w, layer, *, npc, n_sample, t_s):
    blocks = [(M_HEADS, 128, 256), (8, LANES)]
    p_specs, in_state, out_state, y_specs = _mixer_specs(npc, 0, blocks, layer)
    y_shapes, st_shapes = _mixer_out_shapes(npc, n_sample, blocks)
    return pl.pallas_call(
        functools.partial(_mlstm_kernel, npc=npc, t_s=t_s),
        grid=(npc + n_sample // 2,),
        in_specs=p_specs + in_state + [_layer_spec((1, LANES), layer), _layer_spec((1, GROUP), layer)],
        out_specs=y_specs + out_state,
        out_shape=y_shapes + st_shapes,
        scratch_shapes=[pltpu.VMEM((2, M_HEADS, 128, 256), F32), pltpu.VMEM((2, 8, LANES), F32)],
        compiler_params=_cparams(("arbitrary",)),
        name="mlstm",
    )(p3, p3, ce0, ce0, m0, m0, gate_b, norm_w)


def _gla_kernel(pa_ref, pb_ref, sa_ref, sb_ref, up_ref, gkb_ref, nw_ref,
                yp_ref, ys_ref, soa_ref, sob_ref, s_sc, k_sc, b_sc, *, npc, t_s):
    g, first, last, tlen = _step_info(npc, t_s)
    slots = ((pa_ref, sa_ref, soa_ref), (pb_ref, sb_ref, sob_ref))
    row = _iota((CH, LANES), 0)
    lane = _iota((CH, LANES), 1)
    tril = _tril(CH)
    r128 = _iota((LANES, LANES), 0)
    c128 = _iota((LANES, LANES), 1)
    half_ones = ((r128 // 64) == (c128 // 64)).astype(F32)
    eye = (r128 == c128).astype(F32)
    valid = row < tlen

    @pl.when(first)
    def _():
        for s, (_, s0_ref, _) in enumerate(slots):
            s_sc[s] = s0_ref[...]

    pairs = []
    for s, (p_ref, _, _) in enumerate(slots):
        z = _dotb(p_ref[:, 1536:1664], up_ref[...]) + gkb_ref[...]
        la = _log_sigmoid(z) / G_TAU
        la = jnp.where(jnp.concatenate([valid, valid], axis=1), la, 0.0)
        b = _dot_sel_lhs(tril, la)
        for pair in range(2):
            sl = slice(128 * pair, 128 * pair + 128)
            kp = jnp.where(valid, p_ref[:, 256 + 128 * pair:256 + 128 * pair + 128], 0.0)
            bp = b[:, sl]
            k_sc[s, pair] = kp
            b_sc[s, pair] = bp
            pairs.append(dict(s=s, pair=pair, p_ref=p_ref, qp=p_ref[:, sl] * (G_DK ** -0.5), kp=kp, bp=bp,
                              bl=bp[CH - 1:CH, :]))

    def body(ti, accs):
        r0 = pl.multiple_of(ti * 8, 8)
        sums = []
        for u in pairs:
            k8 = k_sc[u['s'], u['pair'], pl.ds(r0, 8), :]
            b8 = b_sc[u['s'], u['pair'], pl.ds(r0, 8), :]
            e = [u['qp'] * k8[j:j + 1, :] * jnp.exp(jnp.minimum(u['bp'] - b8[j:j + 1, :], 0.0)) for j in range(8)]
            sums.append(_dot_sel_rhs(jnp.concatenate(e, axis=0), half_ones))
        out = []
        for acc, sm in zip(accs, sums):
            for j in range(8):
                acc = jnp.where((lane % 64) == r0 + j, sm[CH * j:CH * (j + 1)], acc)
            out.append(acc)
        return tuple(out)

    atts = lax.fori_loop(0, CH // 8, body, tuple(jnp.zeros((CH, LANES), F32) for _ in pairs))
    units = []
    for u, att in zip(pairs, atts):
        att = jnp.where((lane % 64) <= row, att, 0.0)
        decay = _row_to_col(jnp.exp(u['bl']), eye)
        for half in range(2):
            h = 2 * u['pair'] + half
            hm = (lane // 64) == half
            v_h = u['p_ref'][:, 512 + 128 * h:512 + 128 * h + 128]
            units.append(dict(s=u['s'], h=h, v_h=v_h, decay=decay, p_ref=u['p_ref'],
                              qe=jnp.where(hm, u['qp'] * jnp.exp(u['bp']), 0.0), att=jnp.where(hm, att, 0.0),
                              khat=jnp.where(hm, u['kp'] * jnp.exp(u['bl'] - u['bp']), 0.0)))
    sts = [s_sc[u['s'], u['h']] for u in units]
    outs = [_dotb(u['qe'], st) + _dotb(u['att'], jnp.concatenate([u['v_h'], u['v_h']], axis=0))
            for u, st in zip(units, sts)]
    for u, st in zip(units, sts):
        s_sc[u['s'], u['h']] = u['decay'] * st + _dotb_tn(u['khat'], u['v_h'])
    for s, (p_ref, _, so_ref) in enumerate(slots):
        ys = []
        for h in range(G_HEADS):
            o = outs[s * G_HEADS + h]
            on = o * lax.rsqrt(jnp.mean(o * o, axis=-1, keepdims=True) + RMS_EPS) * nw_ref[...]
            gt = p_ref[:, 1024 + 128 * h:1024 + 128 * h + 128]
            ys.append(on * gt * _sigmoid(gt))
        _write_y(g, npc, yp_ref, ys_ref, s, jnp.concatenate(ys, axis=1))

        @pl.when(last)
        def _():
            so_ref[...] = s_sc[s]


def _gla(p3, s0, gk_up, gk_b, norm_w, layer, *, npc, n_sample, t_s):
    blocks = [(G_HEADS, 128, 128)]
    p_specs, in_state, out_state, y_specs = _mixer_specs(npc, 2, blocks, layer)
    y_shapes, st_shapes = _mixer_out_shapes(npc, n_sample, blocks)
    return pl.pallas_call(
        functools.partial(_gla_kernel, npc=npc, t_s=t_s),
        grid=(npc + n_sample // 2,),
        in_specs=p_specs + in_state
        + [_layer_spec((LANES, 256), layer), _layer_spec((1, 256), layer), _layer_spec((1, G_DV), layer)],
        out_specs=y_specs + out_state,
        out_shape=y_shapes + st_shapes,
        scratch_shapes=[pltpu.VMEM((2, G_HEADS, 128, 128), F32), pltpu.VMEM((2, 2, CH, LANES), F32),
                        pltpu.VMEM((2, 2, CH, LANES), F32)],
        compiler_params=_cparams(("arbitrary",)),
        name="gla",
    )(p3, p3, s0, s0, gk_up, gk_b, norm_w)


def _rwkv_kernel(pa_ref, pb_ref, wa_ref, wb_ref, sha_ref, shb_ref, mu_ref, w0_ref, w2_ref, a0_ref, a2_ref,
                 g2_ref, kk_ref, ka_ref, rk_ref, lnw_ref, lnb_ref, bones_ref,
                 yp_ref, ys_ref, woa_ref, wob_ref, shoa_ref, shob_ref, w_sc, sh_sc, *, npc, t_s):
    g, first, last, tlen = _step_info(npc, t_s)
    slots = ((pa_ref, wa_ref, sha_ref, woa_ref, shoa_ref), (pb_ref, wb_ref, shb_ref, wob_ref, shob_ref))
    lane = _iota((CH, LANES), 1)
    tril = _tril(CH)
    r64, c64 = _iota((CH, CH), 0), _iota((CH, CH), 1)
    lower_strict = c64 < r64
    lower = c64 <= r64
    eye64 = (r64 == c64).astype(F32)
    eye128 = (_iota((LANES, LANES), 0) == _iota((LANES, LANES), 1)).astype(F32)
    bones = bones_ref[...]
    valid = _iota((CH, GROUP), 0) < tlen

    @pl.when(first)
    def _():
        for s, (_, w0s_ref, sh0_ref, _, _) in enumerate(slots):
            w_sc[s] = w0s_ref[...]
            sh_sc[s] = sh0_ref[...]

    units, tails = [], []
    for s, (p_ref, _, _, _, _) in enumerate(slots):
        pf = p_ref[...]
        prev = jnp.where(_iota((CH, GW), 0) == 0, sh_sc[s, 0:1, :], pltpu.roll(pf, 1, 0))
        xs = pf + (prev - pf) * mu_ref[...]
        r = xs[:, 0:512]
        k = xs[:, 512:1024]
        v = xs[:, 1024:1536]
        lo = xs[:, 1536:1664]
        wraw = -_softplus(-(w0_ref[...] + _dotb(jnp.tanh(lo), w2_ref[...]))) - 0.5
        logw = jnp.where(valid, -jnp.exp(wraw), 0.0)
        a = _sigmoid(a0_ref[...] + _dotb(lo, a2_ref[...]))
        gg = _dotb(_sigmoid(lo), g2_ref[...])
        kk = k * kk_ref[...]
        kk = kk / jnp.maximum(jnp.sqrt(_dot_sel_rhs(kk * kk, bones)), 1e-12)
        k2 = k * (1.0 + (a - 1.0) * ka_ref[...])
        k2m = jnp.where(valid, k2, 0.0)
        bv = jnp.where(valid, kk * a, 0.0)
        lg = _dot_sel_lhs(tril, logw)
        lgl = lg[CH - 1:CH, :]
        e_out = jnp.exp(-lg)
        e_end = jnp.exp(lgl - lg)
        at = -kk * jnp.exp(lg - logw)
        rt = r * jnp.exp(lg)
        bt = bv * e_out
        kt = k2m * e_out
        bh = bv * e_end
        kh = k2m * e_end
        gl = jnp.exp(lgl)
        tails.append((r, k2, v, gg))
        for pair in range(R_HEADS // 2):
            sl = slice(128 * pair, 128 * pair + 128)
            decay = _row_to_col(gl[:, sl], eye128)
            for half in range(2):
                hm = (lane // 64) == half
                units.append(dict(
                    s=s, h=2 * pair + half, vp=v[:, sl], decay=decay, bt=bt[:, sl], kt=kt[:, sl],
                    ar=jnp.concatenate([jnp.where(hm, at[:, sl], 0.0), jnp.where(hm, rt[:, sl], 0.0)], axis=0),
                    lhs=jnp.concatenate([jnp.where(hm, bh[:, sl], 0.0), jnp.where(hm, kh[:, sl], 0.0)], axis=0)))
    gbs = [_dotb_nt(u['ar'], u['bt']) for u in units]
    gks = [_dotb_nt(u['ar'], u['kt']) for u in units]
    nmat = [jnp.where(lower_strict, gb[0:CH], 0.0) for gb in gbs]
    tms = [eye64 + n for n in nmat]
    mxs = [_dotb(n, n) for n in nmat]
    for it in range(5):
        prods = [_dotb(jnp.concatenate([tm, mx], axis=0), mx) for tm, mx in zip(tms, mxs)]
        tms = [tm + pr[0:CH] for tm, pr in zip(tms, prods)]
        if it < 4:
            mxs = [pr[CH:2 * CH] for pr in prods]
    wsts = [w_sc[u['s'], u['h']] for u in units]
    arws = [_dotb(u['ar'], wst) for u, wst in zip(units, wsts)]
    inner = [_dotb(jnp.where(lower_strict, gk[0:CH], 0.0), u['vp']) for u, gk in zip(units, gks)]
    pms = [_dotb(tm, arw[0:CH] + inn) for tm, arw, inn in zip(tms, arws, inner)]
    yhs = [arw[CH:2 * CH] + _dotb(jnp.where(lower, gb[CH:2 * CH], 0.0), pm)
           + _dotb(jnp.where(lower, gk[CH:2 * CH], 0.0), u['vp'])
           for u, arw, gb, gk, pm in zip(units, arws, gbs, gks, pms)]
    for u, wst, pm in zip(units, wsts, pms):
        w_sc[u['s'], u['h']] = u['decay'] * wst + _dotb_tn(u['lhs'], jnp.concatenate([pm, u['vp']], axis=0))
    for s, (p_ref, _, _, wo_ref, sho_ref) in enumerate(slots):
        r, k2, v, gg = tails[s]
        yh = yhs[s * R_HEADS:(s + 1) * R_HEADS]
        y = jnp.concatenate([jnp.where(lane < 64, yh[2 * pr], yh[2 * pr + 1]) for pr in range(R_HEADS // 2)], axis=1)
        mean = _dot_sel_rhs(y, bones) * (1.0 / R_HEAD)
        d = y - mean
        var = _dot_sel_rhs(d * d, bones) * (1.0 / R_HEAD)
        yn = d * lax.rsqrt(var + R_GN_EPS) * lnw_ref[...] + lnb_ref[...]
        bonus = _dot_sel_rhs(r * k2 * rk_ref[...], bones) * v
        _write_y(g, npc, yp_ref, ys_ref, s, (yn + bonus) * gg)
        last_row = jnp.where(g < npc, p_ref[CH - 1:CH, :], p_ref[t_s - 1:t_s, :])
        sh_sc[s] = jnp.broadcast_to(last_row, (8, GW))

        @pl.when(last)
        def _():
            wo_ref[...] = w_sc[s]
            sho_ref[...] = sh_sc[s]


def _head_block_ones(width, head):
    hid = np.arange(width) // head
    return jnp.asarray((hid[:, None] == hid[None, :]).astype(np.float32))


def _rwkv_consts(mu, w0, w2, a0, a2, g2, k_k, k_a, r_k, ln_w, ln_b):
    depth = mu.shape[0]

    def lora(w, off):
        return jnp.zeros((depth, LANES, GROUP), F32).at[:, off:off + R_LORA].set(w)

    def row(t):
        return t.reshape(depth, 1, GROUP)

    mu_p = jnp.zeros((depth, 1, GW), F32).at[:, 0, :P_R].set(mu)
    return [mu_p, row(w0), lora(w2, 0), row(a0), lora(a2, R_LORA), lora(g2, 2 * R_LORA), row(k_k), row(k_a),
            row(r_k), row(ln_w), row(ln_b)]


def _rwkv(p3, w0s, sh0, consts, layer, *, npc, n_sample, t_s):
    blocks = [(R_HEADS, 128, 128), (8, GW)]
    p_specs, in_state, out_state, y_specs = _mixer_specs(npc, 1, blocks, layer)
    y_shapes, st_shapes = _mixer_out_shapes(npc, n_sample, blocks)
    bones = _head_block_ones(GROUP, R_HEAD)
    return pl.pallas_call(
        functools.partial(_rwkv_kernel, npc=npc, t_s=t_s),
        grid=(npc + n_sample // 2,),
        in_specs=p_specs + in_state + [_layer_spec(c.shape[1:], layer) for c in consts]
        + [pl.BlockSpec(bones.shape, lambda g: (0, 0))],
        out_specs=y_specs + out_state,
        out_shape=y_shapes + st_shapes,
        scratch_shapes=[pltpu.VMEM((2, R_HEADS, 128, 128), F32), pltpu.VMEM((2, 8, GW), F32)],
        compiler_params=_cparams(("arbitrary",)),
        name="rwkv",
    )(p3, p3, w0s, w0s, sh0, sh0, *consts, bones)


def _moba_prep_kernel(p_ref, cos_ref, sin_ref, qn_ref, kn_ref, bones_ref, q_ref, k_ref, kb_ref, vt_ref, km_ref):
    bones = bones_ref[...]
    cos = jnp.concatenate([cos_ref[...]] * 4, axis=1)
    sin = jnp.concatenate([sin_ref[...]] * 4, axis=1)
    lane = _iota(cos.shape, 1)
    low = (lane % A_HEAD) < (A_ROT // 2)

    def norm_rope(x, w):
        ms = _dot_sel_rhs(x * x, bones) * (1.0 / A_HEAD)
        xn = x * lax.rsqrt(ms + RMS_EPS) * w
        partner = jnp.where(low, pltpu.roll(xn, GROUP - A_ROT // 2, 1), pltpu.roll(xn, A_ROT // 2, 1))
        return xn * cos + partner * sin

    q_ref[...] = norm_rope(p_ref[:, 0:512], qn_ref[...])
    kr = norm_rope(p_ref[:, 512:1024], kn_ref[...])
    k_ref[...] = kr
    kb_ref[...] = kr.astype(BF16)
    vt_ref[...] = p_ref[:, 1024:1536].T.astype(BF16)
    km_ref[...] = jnp.mean(kr, axis=0, keepdims=True)


def _moba_prep(p, cos_t, sin_t, q_norm, k_norm):
    rows = p.shape[0]
    nt = rows // MOBA_BLOCK
    rspec = pl.BlockSpec((MOBA_BLOCK, GROUP), lambda i: (i, 0))
    tspec = pl.BlockSpec((MOBA_BLOCK, LANES), lambda i: (i, 0))
    cspec = pl.BlockSpec((1, GROUP), lambda i: (0, 0))
    return pl.pallas_call(
        _moba_prep_kernel,
        grid=(nt,),
        in_specs=[pl.BlockSpec((MOBA_BLOCK, GW), lambda i: (i, 3)), tspec, tspec, cspec, cspec,
                  pl.BlockSpec((GROUP, GROUP), lambda i: (0, 0))],
        out_specs=[rspec, rspec, rspec, pl.BlockSpec((GROUP, MOBA_BLOCK), lambda i: (0, i)),
                   pl.BlockSpec((None, 1, GROUP), lambda i: (i, 0, 0))],
        out_shape=[jax.ShapeDtypeStruct((rows, GROUP), F32), jax.ShapeDtypeStruct((rows, GROUP), F32),
                   jax.ShapeDtypeStruct((rows, GROUP), BF16), jax.ShapeDtypeStruct((GROUP, rows), BF16),
                   jax.ShapeDtypeStruct((nt, 1, GROUP), F32)],
        compiler_params=_cparams(("parallel",)),
        name="moba_prep",
    )(p, cos_t, sin_t, q_norm, k_norm, _head_block_ones(GROUP, A_HEAD))


def _moba_prompt_kernel(q_ref, kb_ref, vt_ref, km_ref, y_ref, *, nb):
    i = pl.program_id(1)
    blk = MOBA_BLOCK
    scale = A_HEAD ** -0.5
    lane = _iota((blk, LANES), 1)
    lane_km = _iota((nb, LANES), 1)
    bidx = _iota((nb, blk), 0)
    causal = _iota((blk, blk), 0) <= _iota((blk, blk), 1)
    row0 = pl.multiple_of(i * blk, blk)
    prs = range(A_HEADS // 2)
    k_own = [kb_ref[pl.ds(row0, blk), 128 * pr:128 * pr + 128] for pr in prs]
    vt_own = [vt_ref[128 * pr:128 * pr + 128, pl.ds(row0, blk)] for pr in prs]
    hp = [(pr, half) for pr in prs for half in range(2)]
    qbs = [(jnp.where((lane // 64) == half, q_ref[:, 128 * pr:128 * pr + 128], 0.0) * scale).astype(BF16)
           for pr, half in hp]
    gates = [_dot_nt(jnp.where((lane_km // 64) == half, km_ref[:, 128 * pr:128 * pr + 128], 0.0),
                     q_ref[:, 128 * pr:128 * pr + 128], HI) for pr, half in hp]
    s_owns = [_dot_nt(k_own[pr], qb) for (pr, _), qb in zip(hp, qbs)]
    heads, p0s = [], []
    for (pr, _), qb, gate, s_own in zip(hp, qbs, gates, s_owns):
        gate = jnp.where(bidx < i, gate, -jnp.inf)
        picks = []
        for _ in range(MOBA_TOPK):
            mx = jnp.max(gate, axis=0, keepdims=True)
            idx = jnp.min(jnp.where(gate == mx, bidx, nb), axis=0, keepdims=True)
            picks.append(jnp.where(mx > -jnp.inf, idx, -1))
            gate = jnp.where(bidx == idx, -jnp.inf, gate)
        s_own = jnp.where(causal, s_own, NEG)
        m0 = jnp.max(s_own, axis=0, keepdims=True)
        p0 = jnp.exp(s_own - m0)
        p0s.append(p0.astype(BF16))
        heads.append(dict(pair=pr, qb=qb, picks=picks, m0=m0, l0=jnp.sum(p0, axis=0, keepdims=True)))
    acc0s = [_dot(vt_own[hd['pair']], p0) for hd, p0 in zip(heads, p0s)]
    inits = tuple((hd['m0'], hd['l0'], acc0) for hd, acc0 in zip(heads, acc0s))

    def body(j, carry):
        c0 = pl.multiple_of(j * blk, blk)
        kjs = [kb_ref[pl.ds(c0, blk), 128 * pr:128 * pr + 128] for pr in range(A_HEADS // 2)]
        vtjs = [vt_ref[128 * pr:128 * pr + 128, pl.ds(c0, blk)] for pr in range(A_HEADS // 2)]
        scores = [_dot_nt(kjs[hd['pair']], hd['qb']) for hd in heads]
        sts = []
        for hd, (m, l, acc), sc in zip(heads, carry, scores):
            picks = hd['picks']
            sel = jnp.logical_or(jnp.logical_or(picks[0] == j, picks[1] == j), picks[2] == j)
            sj = jnp.where(sel, sc, NEG)
            m_new = jnp.maximum(m, jnp.max(sj, axis=0, keepdims=True))
            alpha = jnp.exp(m - m_new)
            pj = jnp.exp(sj - m_new)
            sts.append((m_new, alpha, alpha * l + jnp.sum(pj, axis=0, keepdims=True), pj.astype(BF16)))
        pvs = [_dot(vtjs[hd['pair']], st[3]) for hd, st in zip(heads, sts)]
        return tuple((st[0], st[2], st[1] * acc + pv) for st, (_, _, acc), pv in zip(sts, carry, pvs))

    final = lax.fori_loop(0, i, body, inits)
    outs = [acc / l for _, l, acc in final]
    for pair in range(A_HEADS // 2):
        both = jnp.concatenate([outs[2 * pair][0:64], outs[2 * pair + 1][64:128]], axis=0)
        y_ref[:, 128 * pair:128 * pair + 128] = both.T


def _moba_prompt(q, kb, vt, km, *, nbatch, t):
    nb = t // MOBA_BLOCK
    return pl.pallas_call(
        functools.partial(_moba_prompt_kernel, nb=nb),
        grid=(nbatch, nb),
        in_specs=[pl.BlockSpec((MOBA_BLOCK, GROUP), lambda b, i: (b * nb + i, 0)),
                  pl.BlockSpec((t, GROUP), lambda b, i: (b, 0)),
                  pl.BlockSpec((GROUP, t), lambda b, i: (0, b)),
                  pl.BlockSpec((None, nb, GROUP), lambda b, i: (b, 0, 0))],
        out_specs=pl.BlockSpec((MOBA_BLOCK, GROUP), lambda b, i: (b * nb + i, 0)),
        out_shape=jax.ShapeDtypeStruct((nbatch * t, GROUP), F32),
        compiler_params=_cparams(("parallel", "arbitrary")),
        name="moba_prompt",
    )(q, kb, vt, km)


def _moba_gate_kernel(pt_ref, *refs, npages_step, nblocks, page):
    del pt_ref
    pg_refs = refs[:npages_step]
    qb_ref, sel_ref, g_sc = refs[npages_step:]
    st = pl.program_id(1)
    per = npages_step // 2
    tq = qb_ref.shape[0]
    for i in range(per):
        part = pg_refs[2 * i][...] + pg_refs[2 * i + 1][...]
        for qi in range(tq):
            prod = part * qb_ref[qi]
            g_sc[qi, st * per + i] = jnp.concatenate(
                [jnp.sum(prod[h], axis=0, keepdims=True) for h in range(A_HEADS)], axis=0)

    @pl.when(st == pl.num_programs(1) - 1)
    def _():
        nidx = _iota((nblocks, A_HEADS, 1), 0)
        for qi in range(tq):
            gate = jnp.sum(g_sc[qi], axis=-1, keepdims=True) * (1.0 / (2 * page))
            for slot in range(MOBA_TOPK):
                mx = jnp.max(gate, axis=0, keepdims=True)
                idx = jnp.min(jnp.where(gate == mx, nidx, nblocks), axis=0, keepdims=True)
                sel_ref[qi * MOBA_TOPK + slot] = jnp.broadcast_to(idx[0], (A_HEADS, LANES))
                gate = jnp.where(nidx == idx, -jnp.inf, gate)


def _moba_gate(cache_kt, layer, page_table, qb):
    nseq, npages = page_table.shape
    page = cache_kt.shape[-1]
    nblocks = npages * page // MOBA_BLOCK
    npages_step = 8
    tq = qb.shape[1]

    def pg_spec(i):
        return pl.BlockSpec((None, None, A_HEADS, A_HEAD, page),
                            lambda b, st, pt: (layer, pt[b, st * npages_step + i], 0, 0, 0))

    return pl.pallas_call(
        functools.partial(_moba_gate_kernel, npages_step=npages_step, nblocks=nblocks, page=page),
        grid_spec=pltpu.PrefetchScalarGridSpec(
            num_scalar_prefetch=1,
            grid=(nseq, npages // npages_step),
            in_specs=[pg_spec(i) for i in range(npages_step)]
            + [pl.BlockSpec((None, tq, A_HEADS, A_HEAD, page), lambda b, st, pt: (b, 0, 0, 0, 0))],
            out_specs=pl.BlockSpec((None, tq * MOBA_TOPK, A_HEADS, LANES), lambda b, st, pt: (b, 0, 0, 0)),
            scratch_shapes=[pltpu.VMEM((tq, nblocks, A_HEADS, page), F32)]),
        out_shape=jax.ShapeDtypeStruct((nseq, tq * MOBA_TOPK, A_HEADS, LANES), jnp.int32),
        compiler_params=_cparams(("parallel", "arbitrary")),
        name="moba_gate",
    )(page_table, *([cache_kt] * npages_step), qb)


def _moba_sample_kernel(pg_ref, q_ref, kn_ref, vn_ref, ck_ref, cv_ref, o_ref, kbuf, vbuf, sem, *, layer, tq, page):
    b, h = pl.program_id(0), pl.program_id(1)
    nslab = tq * MOBA_TOPK * (MOBA_BLOCK // page)
    base = (b * pl.num_programs(1) + h) * nslab

    def copies(c):
        pg = pg_ref[base + c]
        return (pltpu.make_async_copy(ck_ref.at[layer, pg, h], kbuf.at[:, pl.ds(c * page, page)], sem.at[0]),
                pltpu.make_async_copy(cv_ref.at[layer, pg, h], vbuf.at[:, pl.ds(c * page, page)], sem.at[1]))

    for c in range(nslab):
        for cp in copies(c):
            cp.start()
    for c in range(nslab):
        for cp in copies(c):
            cp.wait()
    scale = A_HEAD ** -0.5
    nk = nslab * page
    q = q_ref[...]
    rows = q.shape[0]
    owner = _iota((rows, nk), 1) // (MOBA_TOPK * MOBA_BLOCK)
    s_sel = jnp.where(owner == _iota((rows, nk), 0), _dotb(q, kbuf[...]) * scale, NEG)
    r8, c8 = _iota((rows, rows), 0), _iota((rows, rows), 1)
    s_own = jnp.where(jnp.logical_and(c8 <= r8, c8 < tq), _dotb_nt(q, kn_ref[...]) * scale, NEG)
    m = jnp.maximum(jnp.max(s_sel, axis=1, keepdims=True), jnp.max(s_own, axis=1, keepdims=True))
    p_sel = jnp.exp(s_sel - m)
    p_own = jnp.exp(s_own - m)
    l = jnp.sum(p_sel, axis=1, keepdims=True) + jnp.sum(p_own, axis=1, keepdims=True)
    o_ref[...] = (_dotb_nt(p_sel, vbuf[...]) + _dotb(p_own, vn_ref[...])) / l


def _moba_sample(pages, q_s, kn_s, vn_s, cache_kt, cache_vt, layer, *, tq):
    nseq, nh, rows, hd = q_s.shape
    page = cache_kt.shape[-1]
    nslab = tq * MOBA_TOPK * (MOBA_BLOCK // page)
    spec = pl.BlockSpec((None, None, rows, hd), lambda b, h, pg: (b, h, 0, 0))
    return pl.pallas_call(
        functools.partial(_moba_sample_kernel, layer=layer, tq=tq, page=page),
        grid_spec=pltpu.PrefetchScalarGridSpec(
            num_scalar_prefetch=1,
            grid=(nseq, nh),
            in_specs=[spec, spec, spec, pl.BlockSpec(memory_space=pl.ANY), pl.BlockSpec(memory_space=pl.ANY)],
            out_specs=spec,
            scratch_shapes=[pltpu.VMEM((hd, nslab * page), F32), pltpu.VMEM((hd, nslab * page), F32),
                            pltpu.SemaphoreType.DMA((2,))]),
        out_shape=jax.ShapeDtypeStruct((nseq, nh, rows, hd), F32),
        compiler_params=_cparams(("arbitrary", "arbitrary")),
        name="moba_sample",
    )(pages, q_s, kn_s, vn_s, cache_kt, cache_vt)


def _pack_w_in(w):
    def dz(n):
        return jnp.zeros(w.shape[:2] + (n,), w.dtype)

    o_r, o_g, o_a = P_M, P_M + P_R, P_M + P_R + P_G
    kd = 2 * G_HEADS * G_DK + GROUP
    parts = [w[..., 0:P_M], dz(GW - P_M),
             w[..., o_r:o_r + P_R], dz(GW - P_R),
             w[..., o_g:o_g + kd], w[..., o_g + kd + G_LORA:o_g + P_G], w[..., o_g + kd:o_g + kd + G_LORA],
             dz(GW - P_G),
             w[..., o_a:o_a + P_A], dz(GW - P_A)]
    return jnp.concatenate(parts, axis=-1).astype(BF16)


def _even_head(nheads):
    return (jnp.arange(nheads) % 2 == 0).reshape((nheads, 1, 1))


def _place_half(x, nheads, axis):
    z = jnp.zeros_like(x)
    even = _even_head(nheads)
    return jnp.concatenate([jnp.where(even, x, z), jnp.where(even, z, x)], axis=axis)


def _take_half(x, nheads, axis):
    even = _even_head(nheads)
    lo = lax.slice_in_dim(x, 0, 64, axis=x.ndim + axis)
    hi = lax.slice_in_dim(x, 64, 128, axis=x.ndim + axis)
    return jnp.where(even, lo, hi)


def kernel(x_prompt, x_sample, cache_k, cache_v, state_mlstm_c, state_mlstm_n, state_mlstm_m, state_rwkv, state_rwkv_shift, state_gla, page_table, ffn1_norm, ffn1_w_in, ffn1_w_out, mix_norm, w_in, w_out, ffn2_norm, ffn2_w_in, ffn2_w_out, mlstm_gate_b, mlstm_norm_w, rwkv_mu, rwkv_w0, rwkv_w2, rwkv_a0, rwkv_a2, rwkv_g2, rwkv_k_k, rwkv_k_a, rwkv_r_k, rwkv_ln_w, rwkv_ln_b, gla_gk_up, gla_gk_b, gla_norm_w, moba_q_norm, moba_k_norm):
    nb, t, d = x_prompt.shape
    ns, ts, _ = x_sample.shape
    depth = w_in.shape[0]
    assert nb == 2 and ns % 2 == 0 and t % MOBA_BLOCK == 0 and ts <= 8
    npc = t // CH
    n_prompt = nb * t
    rows = n_prompt + ns * CH
    page = cache_k.shape[2]
    past_len = page_table.shape[1] * page
    cache_kt = jnp.transpose(cache_k, (0, 1, 3, 4, 2))
    cache_vt = jnp.transpose(cache_v, (0, 1, 3, 4, 2))
    kw = dict(npc=npc, n_sample=ns, t_s=ts)
    tm = 512 if rows % 512 == 0 else 256

    x = jnp.concatenate([x_prompt.reshape(n_prompt, d),
                         jnp.pad(x_sample, ((0, 0), (0, CH - ts), (0, 0))).reshape(ns * CH, d)], axis=0)
    pos = jnp.concatenate([jnp.tile(jnp.arange(t), nb), jnp.tile(past_len + jnp.arange(CH), ns)])
    inv = ROPE_THETA ** (-(jnp.arange(A_ROT // 2, dtype=F32) * 2.0 / A_ROT))
    ang = pos.astype(F32)[:, None] * inv[None, :]
    one = jnp.ones((rows, A_HEAD - A_ROT), F32)
    cos_h = jnp.concatenate([jnp.cos(ang), jnp.cos(ang), one], axis=1)
    sin_h = jnp.concatenate([-jnp.sin(ang), jnp.sin(ang), 0.0 * one], axis=1)
    cos_t = jnp.concatenate([cos_h, cos_h], axis=1)
    sin_t = jnp.concatenate([sin_h, sin_h], axis=1)

    w_in_p = _pack_w_in(w_in)

    def with_prompt_zeros(st):
        return jnp.concatenate([jnp.zeros((depth, nb) + st.shape[2:], F32), st], axis=1)

    def sample_rows(a, n):
        return a[n_prompt:].reshape(ns, CH, a.shape[-1])[:, :n]

    def heads_major(a):
        return a.reshape(ns, 8, A_HEADS, A_HEAD).transpose(0, 2, 1, 3)

    ce = jnp.concatenate([state_mlstm_c, state_mlstm_n[..., None],
                          jnp.zeros(state_mlstm_n.shape + (LANES - 1,), F32)], axis=-1)
    ce0 = with_prompt_zeros(_place_half(ce, M_HEADS, -2))
    m0 = jnp.zeros((depth, nb + ns, 8, LANES), F32).at[:, nb:, 0, :M_HEADS].set(state_mlstm_m)
    gate_b = jnp.zeros((depth, 1, LANES), F32).at[:, 0, :2 * M_HEADS].set(mlstm_gate_b)
    w0s = with_prompt_zeros(_place_half(_place_half(jnp.swapaxes(state_rwkv, -1, -2), R_HEADS, -2), R_HEADS, -1))
    sh0 = jnp.zeros((depth, nb + ns, 8, GW), F32).at[:, nb:, :, :P_R].set(state_rwkv_shift[:, :, None, :])
    r_consts = _rwkv_consts(rwkv_mu, rwkv_w0, rwkv_w2, rwkv_a0, rwkv_a2, rwkv_g2, rwkv_k_k, rwkv_k_a,
                            rwkv_r_k.reshape(depth, GROUP), rwkv_ln_w, rwkv_ln_b)
    s0 = with_prompt_zeros(_place_half(state_gla, G_HEADS, -2))
    gk_up = jnp.zeros((depth, LANES, G_HEADS * G_DK), F32).at[:, :G_LORA].set(gla_gk_up)
    q_norm = jnp.tile(moba_q_norm, (1, A_HEADS)).reshape(depth, 1, GROUP)
    k_norm = jnp.tile(moba_k_norm, (1, A_HEADS)).reshape(depth, 1, GROUP)
    nbk = t // MOBA_BLOCK
    per_blk = MOBA_BLOCK // page

    raw = []
    for l in range(depth):
        x = _ffn(x, ffn1_norm[l], ffn1_w_in[l].astype(BF16), ffn1_w_out[l].astype(BF16), tm=tm, n_prompt=n_prompt)
        p = _proj(x, mix_norm[l], w_in_p[l], tm=tm, n_prompt=n_prompt)
        p3 = p.reshape(rows // CH, CH, NPK)
        ym_p, ym_s, ce_a, ce_b, m_a, m_b = _mlstm(p3, ce0, m0, gate_b, mlstm_norm_w.reshape(depth, 1, GROUP), l, **kw)
        yr_p, yr_s, w_a, w_b, sh_a, sh_b = _rwkv(p3, w0s, sh0, r_consts, l, **kw)
        yg_p, yg_s, g_a, g_b = _gla(p3, s0, gk_up, gla_gk_b.reshape(depth, 1, -1),
                                    gla_norm_w.reshape(depth, 1, G_DV), l, **kw)

        q_all, k_all, kb, vt, km = _moba_prep(p, cos_t, sin_t, q_norm[l], k_norm[l])
        v_all = p[:, 3 * GW + 2 * GROUP:3 * GW + 3 * GROUP]
        ya_p = _moba_prompt(q_all, kb, vt, km[:nb * nbk].reshape(nb, nbk, GROUP), nbatch=nb, t=t)
        q_s = sample_rows(q_all, 8)
        q_lanes = jnp.broadcast_to(q_s[:, :ts].reshape(ns, ts, A_HEADS, A_HEAD, 1), (ns, ts, A_HEADS, A_HEAD, page))
        sel = _moba_gate(cache_kt, l, page_table, q_lanes)[..., 0]
        sel = sel.reshape(ns, ts, MOBA_TOPK, A_HEADS).transpose(0, 3, 1, 2)
        pidx = per_blk * sel[..., None] + jnp.arange(per_blk)
        pages = page_table[jnp.arange(ns).reshape(ns, 1, 1, 1, 1), pidx].reshape(-1).astype(jnp.int32)
        o_s = _moba_sample(pages, heads_major(q_s), heads_major(sample_rows(k_all, 8)),
                           heads_major(sample_rows(v_all, 8)), cache_kt, cache_vt, l, tq=ts)
        ya_s = jnp.pad(o_s.transpose(0, 2, 1, 3).reshape(ns, 8, GROUP), ((0, 0), (0, CH - 8), (0, 0)))

        x = _outproj(x, [y.reshape(n_prompt, GROUP) for y in (ym_p, yr_p, yg_p, ya_p)],
                     [y.reshape(ns * CH, GROUP) for y in (ym_s, yr_s, yg_s, ya_s)], w_out[l].astype(BF16), tm=tm)
        x = _ffn(x, ffn2_norm[l], ffn2_w_in[l].astype(BF16), ffn2_w_out[l].astype(BF16), tm=tm, n_prompt=n_prompt)
        raw.append((k_all, v_all, ce_a, ce_b, m_a, m_b, w_a, w_b, sh_a, sh_b, g_a, g_b))

    k_all, v_all, ce_a, ce_b, m_a, m_b, w_a, w_b, sh_a, sh_b, g_a, g_b = (
        jnp.stack([lay[i] for lay in raw]) for i in range(len(raw[0])))
    shp_p, shp_s = (depth, nb, t, A_HEADS, A_HEAD), (depth, ns, ts, A_HEADS, A_HEAD)

    def kv_sample(a):
        return a[:, n_prompt:].reshape(depth, ns, CH, GROUP)[:, :, :ts].reshape(shp_s)

    ce_p, ce_s = (_take_half(c, M_HEADS, -2) for c in _merge_state(ce_a, ce_b, nb))
    mm_p, mm_s = (m[:, :, 0, :M_HEADS] for m in _merge_state(m_a, m_b, nb))
    rw_p, rw_s = (jnp.swapaxes(_take_half(_take_half(w, R_HEADS, -2), R_HEADS, -1), -1, -2)
                  for w in _merge_state(w_a, w_b, nb))
    sh_p, sh_s = (sh[:, :, 0, :P_R] for sh in _merge_state(sh_a, sh_b, nb))
    gl_p, gl_s = (_take_half(gs, G_HEADS, -2) for gs in _merge_state(g_a, g_b, nb))
    y_p = x[:n_prompt].reshape(nb, t, d)
    y_s = x[n_prompt:].reshape(ns, CH, d)[:, :ts]
    return (y_p, y_s, k_all[:, :n_prompt].reshape(shp_p), v_all[:, :n_prompt].reshape(shp_p),
            kv_sample(k_all), kv_sample(v_all),
            ce_p[..., :M_DV], ce_s[..., :M_DV], ce_p[..., M_DV], ce_s[..., M_DV], mm_p, mm_s,
            rw_p, rw_s, sh_p, sh_s, gl_p, gl_s)
```

```python
import functools

import jax
import jax.numpy as jnp
import numpy as np
from jax import lax
from jax.experimental import pallas as pl
from jax.experimental.pallas import tpu as pltpu

F32 = jnp.float32
BF16 = jnp.bfloat16
HI = lax.Precision.HIGHEST

D_MODEL = 2048
GROUP = D_MODEL // 4
M_HEADS = 4
M_DV = GROUP // M_HEADS
M_DK = M_DV // 2
GATE_CAP = 15.0
R_HEAD = 64
R_HEADS = GROUP // R_HEAD
R_LORA = 32
R_GN_EPS = 64e-5
G_HEADS = 4
G_DV = GROUP // G_HEADS
G_DK = G_DV // 2
G_LORA = 16
G_TAU = 16.0
A_HEAD = 64
A_HEADS = GROUP // A_HEAD
A_ROT = A_HEAD // 4
ROPE_THETA = 500000.0
MOBA_BLOCK = 256
MOBA_TOPK = 3
RMS_EPS = 1e-6
P_M = 2 * M_HEADS * M_DK + 2 * GROUP + 2 * M_HEADS
P_R = 3 * GROUP + 3 * R_LORA
P_G = 2 * G_HEADS * G_DK + 2 * GROUP + G_LORA
P_A = 3 * GROUP

CH = 64
GW = 1664
NPK = 4 * GW
NEG = -1e30
LANES = 128
VMEM_LIMIT = 48 * 1024 * 1024


def _dot(a, b, prec=None):
    return jnp.dot(a, b, preferred_element_type=F32, precision=prec)


def _dot_nt(a, b, prec=None):
    return lax.dot_general(a, b, (((1,), (1,)), ((), ())), preferred_element_type=F32, precision=prec)


def _dot_tn(a, b, prec=None):
    return lax.dot_general(a, b, (((0,), (0,)), ((), ())), preferred_element_type=F32, precision=prec)


def _bf(x):
    return x.astype(BF16)


def _dotb(a, b):
    return _dot(_bf(a), _bf(b))


def _dotb_nt(a, b):
    return _dot_nt(_bf(a), _bf(b))


def _dotb_tn(a, b):
    return _dot_tn(_bf(a), _bf(b))


def _split2(x):
    hi = x.astype(BF16)
    return hi, (x - hi.astype(F32)).astype(BF16)


def _dot_sel_rhs(a, sel):
    hi, lo = _split2(a)
    sel = _bf(sel)
    return _dot(hi, sel) + _dot(lo, sel)


def _dot_sel_lhs(sel, b):
    hi, lo = _split2(b)
    sel = _bf(sel)
    return _dot(sel, hi) + _dot(sel, lo)


def _row_to_col(row, eye):
    return jnp.sum(eye * row, axis=1, keepdims=True)


def _iota(shape, dim):
    return lax.broadcasted_iota(jnp.int32, shape, dim)


def _sigmoid(x):
    return 1.0 / (1.0 + jnp.exp(-x))


def _softplus(x):
    return jnp.maximum(x, 0.0) + jnp.log(1.0 + jnp.exp(-jnp.abs(x)))


def _log_sigmoid(x):
    return -_softplus(-x)


def _tril(n):
    return (_iota((n, n), 1) <= _iota((n, n), 0)).astype(F32)


def _cparams(sem, vmem=VMEM_LIMIT):
    return pltpu.CompilerParams(dimension_semantics=sem, vmem_limit_bytes=vmem)


def _rms_rows(x, w):
    return x * lax.rsqrt(jnp.mean(x * x, axis=-1, keepdims=True) + RMS_EPS) * w


def _sample_rows(ref, rows_per_slot):
    tm, c = ref.shape
    return ref[...].reshape(tm // CH, CH, c)[:, 0:rows_per_slot, :].reshape(tm // CH * rows_per_slot, c)


def _scatter_sample_rows(o_ref, vals, rows_per_slot):
    for b in range(o_ref.shape[0] // CH):
        o_ref[CH * b:CH * b + rows_per_slot, :] = vals[rows_per_slot * b:rows_per_slot * (b + 1), :]


def _ffn_kernel(x_ref, nw_ref, wg_ref, wu_ref, wo_ref, o_ref, h_ref, acc_ref, *, prompt_tiles, ms):
    i, j = pl.program_id(0), pl.program_id(1)
    last = j == pl.num_programs(1) - 1

    def swiglu_part(h):
        gate = _dot(h, wg_ref[...])
        up = _dot(h, wu_ref[...])
        return _dot((gate * _sigmoid(gate) * up).astype(BF16), wo_ref[...])

    @pl.when(i < prompt_tiles)
    def _():
        @pl.when(j == 0)
        def _():
            h_ref[...] = _rms_rows(x_ref[...], nw_ref[...]).astype(BF16)
            acc_ref[...] = jnp.zeros_like(acc_ref)

        acc_ref[...] += swiglu_part(h_ref[...])

        @pl.when(last)
        def _():
            o_ref[...] = x_ref[...] + 0.5 * acc_ref[...]

    @pl.when(i >= prompt_tiles)
    def _():
        @pl.when(j == 0)
        def _():
            h_ref[0:ms, :] = _rms_rows(_sample_rows(x_ref, 8), nw_ref[...]).astype(BF16)
            acc_ref[0:ms, :] = jnp.zeros((ms, acc_ref.shape[1]), F32)

        acc_ref[0:ms, :] += swiglu_part(h_ref[0:ms, :])

        @pl.when(last)
        def _():
            o_ref[...] = x_ref[...]
            _scatter_sample_rows(o_ref, _sample_rows(x_ref, 8) + 0.5 * acc_ref[0:ms, :], 8)


def _ffn(x, norm_w, w_in, w_out, layer, *, tm, n_prompt, tf=512):
    rows, d = x.shape
    dff = w_out.shape[1]
    nj = dff // tf
    return pl.pallas_call(
        functools.partial(_ffn_kernel, prompt_tiles=n_prompt // tm, ms=tm // CH * 8),
        grid=(rows // tm, nj),
        in_specs=[
            pl.BlockSpec((tm, d), lambda i, j: (i, 0)),
            pl.BlockSpec((1, d), lambda i, j: (0, 0)),
            pl.BlockSpec((None, d, tf), lambda i, j: (layer, 0, j)),
            pl.BlockSpec((None, d, tf), lambda i, j: (layer, 0, j + nj)),
            pl.BlockSpec((None, tf, d), lambda i, j: (layer, j, 0)),
        ],
        out_specs=pl.BlockSpec((tm, d), lambda i, j: (i, 0)),
        out_shape=jax.ShapeDtypeStruct((rows, d), F32),
        scratch_shapes=[pltpu.VMEM((tm, d), BF16), pltpu.VMEM((tm, d), F32)],
        compiler_params=_cparams(("parallel", "arbitrary")),
        name="ffn",
    )(x, norm_w.reshape(1, d), w_in, w_in, w_out)


def _proj_kernel(x_ref, nw_ref, w_ref, o_ref, h_ref, *, prompt_tiles, ms):
    i, j = pl.program_id(0), pl.program_id(1)

    @pl.when(i < prompt_tiles)
    def _():
        @pl.when(j == 0)
        def _():
            h_ref[...] = _rms_rows(x_ref[...], nw_ref[...]).astype(BF16)

        o_ref[...] = _dot(h_ref[...], w_ref[...])

    @pl.when(i >= prompt_tiles)
    def _():
        @pl.when(j == 0)
        def _():
            h_ref[0:ms, :] = _rms_rows(_sample_rows(x_ref, 8), nw_ref[...]).astype(BF16)

        o_ref[...] = jnp.zeros_like(o_ref)
        _scatter_sample_rows(o_ref, _dot(h_ref[0:ms, :], w_ref[...]), 8)


def _proj(x, norm_w, w, layer, *, tm, n_prompt, tn=GW):
    rows, d = x.shape
    n = w.shape[2]
    return pl.pallas_call(
        functools.partial(_proj_kernel, prompt_tiles=n_prompt // tm, ms=tm // CH * 8),
        grid=(rows // tm, n // tn),
        in_specs=[
            pl.BlockSpec((tm, d), lambda i, j: (i, 0)),
            pl.BlockSpec((1, d), lambda i, j: (0, 0)),
            pl.BlockSpec((None, d, tn), lambda i, j: (layer, 0, j)),
        ],
        out_specs=pl.BlockSpec((tm, tn), lambda i, j: (i, j)),
        out_shape=jax.ShapeDtypeStruct((rows, n), F32),
        scratch_shapes=[pltpu.VMEM((tm, d), BF16)],
        compiler_params=_cparams(("parallel", "arbitrary")),
        name="proj",
    )(x, norm_w.reshape(1, d), w)


def _outproj_kernel(x_ref, *refs, prompt_tiles):
    yp_refs, ys_refs, w_ref, o_ref = refs[0:4], refs[4:8], refs[8], refs[9]

    def run(y_refs):
        acc = x_ref[...]
        for gi, y_ref in enumerate(y_refs):
            acc = acc + _dot(y_ref[...].astype(BF16), w_ref[gi * GROUP:(gi + 1) * GROUP, :])
        o_ref[...] = acc

    @pl.when(pl.program_id(0) < prompt_tiles)
    def _():
        run(yp_refs)

    @pl.when(pl.program_id(0) >= prompt_tiles)
    def _():
        run(ys_refs)


def _outproj(x, y_prompt, y_sample, w, layer, *, tm):
    rows, d = x.shape
    pt = y_prompt[0].shape[0] // tm
    p_spec = pl.BlockSpec((tm, GROUP), lambda i: (jnp.minimum(i, pt - 1), 0))
    s_spec = pl.BlockSpec((tm, GROUP), lambda i: (jnp.maximum(i - pt, 0), 0))
    return pl.pallas_call(
        functools.partial(_outproj_kernel, prompt_tiles=pt),
        grid=(rows // tm,),
        in_specs=[pl.BlockSpec((tm, d), lambda i: (i, 0))] + [p_spec] * 4 + [s_spec] * 4
        + [pl.BlockSpec((None, d, d), lambda i: (layer, 0, 0))],
        out_specs=pl.BlockSpec((tm, d), lambda i: (i, 0)),
        out_shape=jax.ShapeDtypeStruct((rows, d), F32),
        compiler_params=_cparams(("parallel",)),
        name="outproj",
    )(x, *y_prompt, *y_sample, w)


def _layer_spec(shape, layer):
    return pl.BlockSpec((None,) + shape, lambda g: (layer,) + (0,) * len(shape))


def _mixer_specs(npc, group, state_blocks, layer):
    def chunk_idx(s):
        return lambda g: (jnp.where(g < npc, s * npc + g, 2 * npc + 2 * (g - npc) + s), 0, group)

    p_specs = [pl.BlockSpec((None, CH, GW), chunk_idx(s)) for s in (0, 1)]
    in_state, out_state = [], []
    for blk in state_blocks:
        zeros = (0,) * len(blk)
        for s in (0, 1):
            in_state.append(pl.BlockSpec(
                (None, None) + blk,
                lambda g, s=s, z=zeros: (layer, jnp.where(g < npc, s, 2 + 2 * (g - npc) + s)) + z))
            out_state.append(pl.BlockSpec(
                (None,) + blk, lambda g, z=zeros: (jnp.where(g < npc, 0, 1 + g - npc),) + z))
    y_specs = [pl.BlockSpec((2, None, CH, GROUP), lambda g: (0, jnp.minimum(g, npc - 1), 0, 0)),
               pl.BlockSpec((None, 2, CH, GROUP), lambda g: (jnp.maximum(g - npc, 0), 0, 0, 0))]
    return p_specs, in_state, out_state, y_specs


def _mixer_out_shapes(npc, n_sample, state_blocks):
    ys = [jax.ShapeDtypeStruct((2, npc, CH, GROUP), F32), jax.ShapeDtypeStruct((n_sample // 2, 2, CH, GROUP), F32)]
    st = []
    for blk in state_blocks:
        st += [jax.ShapeDtypeStruct((1 + n_sample // 2,) + blk, F32)] * 2
    return ys, st


def _step_info(npc, t_s):
    g = pl.program_id(0)
    first = jnp.logical_or(g == 0, g >= npc)
    last = g >= npc - 1
    tlen = jnp.where(g < npc, CH, t_s)
    return g, first, last, tlen


def _write_y(g, npc, yp_ref, ys_ref, s, val):
    @pl.when(g < npc)
    def _():
        yp_ref[s] = val

    @pl.when(g >= npc)
    def _():
        ys_ref[s] = val


def _merge_state(a, b, nb):
    assert nb == 2
    prompt = jnp.stack([a[:, 0], b[:, 0]], axis=1)
    sample = jnp.stack([a[:, 1:], b[:, 1:]], axis=2).reshape((a.shape[0], -1) + a.shape[2:])
    return prompt, sample


def _mlstm_kernel(pa_ref, pb_ref, cea_ref, ceb_ref, ma_ref, mb_ref, gb_ref, nw_ref,
                  yp_ref, ys_ref, ceoa_ref, ceob_ref, moa_ref, mob_ref, ce_sc, m_sc, *, npc, t_s):
    g, first, last, tlen = _step_info(npc, t_s)
    slots = ((pa_ref, cea_ref, ma_ref, ceoa_ref, moa_ref), (pb_ref, ceb_ref, mb_ref, ceob_ref, mob_ref))
    row = _iota((CH, LANES), 0)
    lane = _iota((CH, LANES), 1)
    tril = _tril(CH)
    causal = _iota((CH, CH), 1) <= _iota((CH, CH), 0)
    e0 = (lane == 0).astype(F32)
    valid = row < tlen

    @pl.when(first)
    def _():
        for s, (_, ce0_ref, m0_ref, _, _) in enumerate(slots):
            ce_sc[s] = ce0_ref[...]
            m_sc[s] = m0_ref[...]

    units = []
    for s, (p_ref, _, _, _, _) in enumerate(slots):
        gates = p_ref[:, 1536:1664] + gb_ref[...]
        gates = GATE_CAP * jnp.tanh(gates / GATE_CAP)
        ig = jnp.where(valid, gates, NEG)
        lf = jnp.where(valid, _log_sigmoid(gates), 0.0)
        b_col = _dot_sel_lhs(tril, lf)
        ig_t = ig.T
        lf_hi, lf_lo = _split2(lf.T[0:8])
        b_row = _dot_nt(lf_hi, _bf(tril)) + _dot_nt(lf_lo, _bf(tril))
        for h in range(M_HEADS):
            pair, half = h // 2, h % 2
            hm = (lane // 64) == half
            b_c = b_col[:, 4 + h:5 + h]
            m_h = m_sc[s, 0:1, h:h + 1]
            dmat = jnp.where(causal, b_c - b_row[4 + h:5 + h, :] + ig_t[h:h + 1, :], NEG)
            m_inter = b_c + m_h
            mt = jnp.maximum(m_inter, jnp.max(dmat, axis=1, keepdims=True))
            m_new = mt[CH - 1:CH, :]
            b_last = b_c[CH - 1:CH, :]
            units.append(dict(
                s=s, h=h, p_ref=p_ref, mt=mt, m_new=m_new, dexp=jnp.exp(dmat - mt), s_inter=jnp.exp(m_inter - mt),
                carry=jnp.exp(b_last + m_h - m_new), ws=jnp.exp(b_last - b_c + ig[:, h:h + 1] - m_new),
                qm=jnp.where(hm, p_ref[:, 128 * pair:128 * pair + 128], 0.0),
                km=jnp.where(hm, p_ref[:, 256 + 128 * pair:256 + 128 * pair + 128], 0.0) * (M_DK ** -0.5),
                v_ext=jnp.concatenate([p_ref[:, 512 + 128 * h:512 + 128 * h + 128], e0], axis=1)))
    ces = [ce_sc[u['s'], u['h']] for u in units]
    qks = [_dotb_nt(u['qm'], u['km']) for u in units]
    qcs = [_dotb(u['qm'], ce) for u, ce in zip(units, ces)]
    nds = [u['s_inter'] * qc + _dotb(u['dexp'] * qk, u['v_ext']) for u, qk, qc in zip(units, qks, qcs)]
    for u, ce in zip(units, ces):
        ce_sc[u['s'], u['h']] = u['carry'] * ce + _dotb_tn(u['km'], u['ws'] * u['v_ext'])
        m_sc[u['s'], 0:1, u['h']:u['h'] + 1] = u['m_new']
    for s, (p_ref, _, _, ceo_ref, mo_ref) in enumerate(slots):
        ys = []
        for h in range(M_HEADS):
            u, nd = units[s * M_HEADS + h], nds[s * M_HEADS + h]
            hh = nd[:, 0:128] / jnp.maximum(jnp.abs(nd[:, 128:129]), jnp.exp(-u['mt']))
            hn = hh * lax.rsqrt(jnp.mean(hh * hh, axis=-1, keepdims=True) + RMS_EPS)
            hn = hn * nw_ref[:, 128 * h:128 * h + 128]
            ys.append(hn * _sigmoid(p_ref[:, 1024 + 128 * h:1024 + 128 * h + 128]))
        _write_y(g, npc, yp_ref, ys_ref, s, jnp.concatenate(ys, axis=1))

        @pl.when(last)
        def _():
            ceo_ref[...] = ce_sc[s]
            mo_ref[...] = m_sc[s]


def _mlstm(p3, ce0, m0, gate_b, norm_w, layer, *, npc, n_sample, t_s):
    blocks = [(M_HEADS, 128, 256), (8, LANES)]
    p_specs, in_state, out_state, y_specs = _mixer_specs(npc, 0, blocks, layer)
    y_shapes, st_shapes = _mixer_out_shapes(npc, n_sample, blocks)
    return pl.pallas_call(
        functools.partial(_mlstm_kernel, npc=npc, t_s=t_s),
        grid=(npc + n_sample // 2,),
        in_specs=p_specs + in_state + [_layer_spec((1, LANES), layer), _layer_spec((1, GROUP), layer)],
        out_specs=y_specs + out_state,
        out_shape=y_shapes + st_shapes,
        scratch_shapes=[pltpu.VMEM((2, M_HEADS, 128, 256), F32), pltpu.VMEM((2, 8, LANES), F32)],
        compiler_params=_cparams(("arbitrary",)),
        name="mlstm",
    )(p3, p3, ce0, ce0, m0, m0, gate_b, norm_w)


def _gla_kernel(pa_ref, pb_ref, sa_ref, sb_ref, up_ref, gkb_ref, nw_ref,
                yp_ref, ys_ref, soa_ref, sob_ref, s_sc, k_sc, b_sc, *, npc, t_s):
    g, first, last, tlen = _step_info(npc, t_s)
    slots = ((pa_ref, sa_ref, soa_ref), (pb_ref, sb_ref, sob_ref))
    row = _iota((CH, LANES), 0)
    lane = _iota((CH, LANES), 1)
    tril = _tril(CH)
    r128 = _iota((LANES, LANES), 0)
    c128 = _iota((LANES, LANES), 1)
    half_ones = ((r128 // 64) == (c128 // 64)).astype(F32)
    eye = (r128 == c128).astype(F32)
    valid = row < tlen

    @pl.when(first)
    def _():
        for s, (_, s0_ref, _) in enumerate(slots):
            s_sc[s] = s0_ref[...]

    pairs = []
    for s, (p_ref, _, _) in enumerate(slots):
        z = _dotb(p_ref[:, 1536:1664], up_ref[...]) + gkb_ref[...]
        la = _log_sigmoid(z) / G_TAU
        la = jnp.where(jnp.concatenate([valid, valid], axis=1), la, 0.0)
        b = _dot_sel_lhs(tril, la)
        for pair in range(2):
            sl = slice(128 * pair, 128 * pair + 128)
            kp = jnp.where(valid, p_ref[:, 256 + 128 * pair:256 + 128 * pair + 128], 0.0)
            bp = b[:, sl]
            k_sc[s, pair] = kp
            b_sc[s, pair] = bp
            pairs.append(dict(s=s, pair=pair, p_ref=p_ref, qp=p_ref[:, sl] * (G_DK ** -0.5), kp=kp, bp=bp,
                              bl=bp[CH - 1:CH, :]))

    def body(ti, accs):
        r0 = pl.multiple_of(ti * 8, 8)
        sums = []
        for u in pairs:
            k8 = k_sc[u['s'], u['pair'], pl.ds(r0, 8), :]
            b8 = b_sc[u['s'], u['pair'], pl.ds(r0, 8), :]
            e = [u['qp'] * k8[j:j + 1, :] * jnp.exp(jnp.minimum(u['bp'] - b8[j:j + 1, :], 0.0)) for j in range(8)]
            sums.append(_dotb(jnp.concatenate(e, axis=0), half_ones))
        out = []
        for acc, sm in zip(accs, sums):
            for j in range(8):
                acc = jnp.where((lane % 64) == r0 + j, sm[CH * j:CH * (j + 1)], acc)
            out.append(acc)
        return tuple(out)

    atts = lax.fori_loop(0, CH // 8, body, tuple(jnp.zeros((CH, LANES), F32) for _ in pairs))
    units = []
    for u, att in zip(pairs, atts):
        att = jnp.where((lane % 64) <= row, att, 0.0)
        decay = _row_to_col(jnp.exp(u['bl']), eye)
        for half in range(2):
            h = 2 * u['pair'] + half
            hm = (lane // 64) == half
            v_h = u['p_ref'][:, 512 + 128 * h:512 + 128 * h + 128]
            units.append(dict(s=u['s'], h=h, v_h=v_h, decay=decay, p_ref=u['p_ref'],
                              qe=jnp.where(hm, u['qp'] * jnp.exp(u['bp']), 0.0), att=jnp.where(hm, att, 0.0),
                              khat=jnp.where(hm, u['kp'] * jnp.exp(u['bl'] - u['bp']), 0.0)))
    sts = [s_sc[u['s'], u['h']] for u in units]
    outs = [_dotb(u['qe'], st) + _dotb(u['att'], jnp.concatenate([u['v_h'], u['v_h']], axis=0))
            for u, st in zip(units, sts)]
    for u, st in zip(units, sts):
        s_sc[u['s'], u['h']] = u['decay'] * st + _dotb_tn(u['khat'], u['v_h'])
    for s, (p_ref, _, so_ref) in enumerate(slots):
        ys = []
        for h in range(G_HEADS):
            o = outs[s * G_HEADS + h]
            on = o * lax.rsqrt(jnp.mean(o * o, axis=-1, keepdims=True) + RMS_EPS) * nw_ref[...]
            gt = p_ref[:, 1024 + 128 * h:1024 + 128 * h + 128]
            ys.append(on * gt * _sigmoid(gt))
        _write_y(g, npc, yp_ref, ys_ref, s, jnp.concatenate(ys, axis=1))

        @pl.when(last)
        def _():
            so_ref[...] = s_sc[s]


def _gla(p3, s0, gk_up, gk_b, norm_w, layer, *, npc, n_sample, t_s):
    blocks = [(G_HEADS, 128, 128)]
    p_specs, in_state, out_state, y_specs = _mixer_specs(npc, 2, blocks, layer)
    y_shapes, st_shapes = _mixer_out_shapes(npc, n_sample, blocks)
    return pl.pallas_call(
        functools.partial(_gla_kernel, npc=npc, t_s=t_s),
        grid=(npc + n_sample // 2,),
        in_specs=p_specs + in_state
        + [_layer_spec((LANES, 256), layer), _layer_spec((1, 256), layer), _layer_spec((1, G_DV), layer)],
        out_specs=y_specs + out_state,
        out_shape=y_shapes + st_shapes,
        scratch_shapes=[pltpu.VMEM((2, G_HEADS, 128, 128), F32), pltpu.VMEM((2, 2, CH, LANES), F32),
                        pltpu.VMEM((2, 2, CH, LANES), F32)],
        compiler_params=_cparams(("arbitrary",)),
        name="gla",
    )(p3, p3, s0, s0, gk_up, gk_b, norm_w)


def _rwkv_kernel(pa_ref, pb_ref, wa_ref, wb_ref, sha_ref, shb_ref, mu_ref, w0_ref, w2_ref, a0_ref, a2_ref,
                 g2_ref, kk_ref, ka_ref, rk_ref, lnw_ref, lnb_ref, bones_ref,
                 yp_ref, ys_ref, woa_ref, wob_ref, shoa_ref, shob_ref, w_sc, sh_sc, *, npc, t_s):
    g, first, last, tlen = _step_info(npc, t_s)
    slots = ((pa_ref, wa_ref, sha_ref, woa_ref, shoa_ref), (pb_ref, wb_ref, shb_ref, wob_ref, shob_ref))
    lane = _iota((CH, LANES), 1)
    tril = _tril(CH)
    r64, c64 = _iota((CH, CH), 0), _iota((CH, CH), 1)
    lower_strict = c64 < r64
    lower = c64 <= r64
    eye64 = (r64 == c64).astype(F32)
    eye128 = (_iota((LANES, LANES), 0) == _iota((LANES, LANES), 1)).astype(F32)
    bones = bones_ref[...]
    valid = _iota((CH, GROUP), 0) < tlen

    @pl.when(first)
    def _():
        for s, (_, w0s_ref, sh0_ref, _, _) in enumerate(slots):
            w_sc[s] = w0s_ref[...]
            sh_sc[s] = sh0_ref[...]

    units, tails = [], []
    for s, (p_ref, _, _, _, _) in enumerate(slots):
        pf = p_ref[...]
        prev = jnp.where(_iota((CH, GW), 0) == 0, sh_sc[s, 0:1, :], pltpu.roll(pf, 1, 0))
        xs = pf + (prev - pf) * mu_ref[...]
        r = xs[:, 0:512]
        k = xs[:, 512:1024]
        v = xs[:, 1024:1536]
        lo = xs[:, 1536:1664]
        wraw = -_softplus(-(w0_ref[...] + _dotb(jnp.tanh(lo), w2_ref[...]))) - 0.5
        logw = jnp.where(valid, -jnp.exp(wraw), 0.0)
        a = _sigmoid(a0_ref[...] + _dotb(lo, a2_ref[...]))
        gg = _dotb(_sigmoid(lo), g2_ref[...])
        kk = k * kk_ref[...]
        kk = kk / jnp.maximum(jnp.sqrt(_dot_sel_rhs(kk * kk, bones)), 1e-12)
        k2 = k * (1.0 + (a - 1.0) * ka_ref[...])
        k2m = jnp.where(valid, k2, 0.0)
        bv = jnp.where(valid, kk * a, 0.0)
        lg = _dot_sel_lhs(tril, logw)
        lgl = lg[CH - 1:CH, :]
        e_out = jnp.exp(-lg)
        e_end = jnp.exp(lgl - lg)
        at = -kk * jnp.exp(lg - logw)
        rt = r * jnp.exp(lg)
        bt = bv * e_out
        kt = k2m * e_out
        bh = bv * e_end
        kh = k2m * e_end
        gl = jnp.exp(lgl)
        tails.append((r, k2, v, gg))
        for pair in range(R_HEADS // 2):
            sl = slice(128 * pair, 128 * pair + 128)
            decay = _row_to_col(gl[:, sl], eye128)
            for half in range(2):
                hm = (lane // 64) == half
                units.append(dict(
                    s=s, h=2 * pair + half, vp=v[:, sl], decay=decay, bt=bt[:, sl], kt=kt[:, sl],
                    ar=jnp.concatenate([jnp.where(hm, at[:, sl], 0.0), jnp.where(hm, rt[:, sl], 0.0)], axis=0),
                    lhs=jnp.concatenate([jnp.where(hm, bh[:, sl], 0.0), jnp.where(hm, kh[:, sl], 0.0)], axis=0)))
    gbs = [_dotb_nt(u['ar'], u['bt']) for u in units]
    gks = [_dotb_nt(u['ar'], u['kt']) for u in units]
    nmat = [jnp.where(lower_strict, gb[0:CH], 0.0) for gb in gbs]
    tms = [eye64 + n for n in nmat]
    mxs = [_dotb(n, n) for n in nmat]
    for it in range(5):
        prods = [_dotb(jnp.concatenate([tm, mx], axis=0), mx) for tm, mx in zip(tms, mxs)]
        tms = [tm + pr[0:CH] for tm, pr in zip(tms, prods)]
        if it < 4:
            mxs = [pr[CH:2 * CH] for pr in prods]
    wsts = [w_sc[u['s'], u['h']] for u in units]
    arws = [_dotb(u['ar'], wst) for u, wst in zip(units, wsts)]
    inner = [_dotb(jnp.where(lower_strict, gk[0:CH], 0.0), u['vp']) for u, gk in zip(units, gks)]
    pms = [_dotb(tm, arw[0:CH] + inn) for tm, arw, inn in zip(tms, arws, inner)]
    yhs = [arw[CH:2 * CH] + _dotb(jnp.where(lower, gb[CH:2 * CH], 0.0), pm)
           + _dotb(jnp.where(lower, gk[CH:2 * CH], 0.0), u['vp'])
           for u, arw, gb, gk, pm in zip(units, arws, gbs, gks, pms)]
    for u, wst, pm in zip(units, wsts, pms):
        w_sc[u['s'], u['h']] = u['decay'] * wst + _dotb_tn(u['lhs'], jnp.concatenate([pm, u['vp']], axis=0))
    for s, (p_ref, _, _, wo_ref, sho_ref) in enumerate(slots):
        r, k2, v, gg = tails[s]
        yh = yhs[s * R_HEADS:(s + 1) * R_HEADS]
        y = jnp.concatenate([jnp.where(lane < 64, yh[2 * pr], yh[2 * pr + 1]) for pr in range(R_HEADS // 2)], axis=1)
        mean = _dot_sel_rhs(y, bones) * (1.0 / R_HEAD)
        d = y - mean
        var = _dot_sel_rhs(d * d, bones) * (1.0 / R_HEAD)
        yn = d * lax.rsqrt(var + R_GN_EPS) * lnw_ref[...] + lnb_ref[...]
        bonus = _dot_sel_rhs(r * k2 * rk_ref[...], bones) * v
        _write_y(g, npc, yp_ref, ys_ref, s, (yn + bonus) * gg)
        last_row = jnp.where(g < npc, p_ref[CH - 1:CH, :], p_ref[t_s - 1:t_s, :])
        sh_sc[s] = jnp.broadcast_to(last_row, (8, GW))

        @pl.when(last)
        def _():
            wo_ref[...] = w_sc[s]
            sho_ref[...] = sh_sc[s]


def _head_block_ones(width, head):
    hid = np.arange(width) // head
    return jnp.asarray((hid[:, None] == hid[None, :]).astype(np.float32))


def _rwkv_consts(mu, w0, w2, a0, a2, g2, k_k, k_a, r_k, ln_w, ln_b):
    depth = mu.shape[0]

    def lora(w, off):
        return jnp.zeros((depth, LANES, GROUP), F32).at[:, off:off + R_LORA].set(w)

    def row(t):
        return t.reshape(depth, 1, GROUP)

    mu_p = jnp.zeros((depth, 1, GW), F32).at[:, 0, :P_R].set(mu)
    return [mu_p, row(w0), lora(w2, 0), row(a0), lora(a2, R_LORA), lora(g2, 2 * R_LORA), row(k_k), row(k_a),
            row(r_k), row(ln_w), row(ln_b)]


def _rwkv(p3, w0s, sh0, consts, layer, *, npc, n_sample, t_s):
    blocks = [(R_HEADS, 128, 128), (8, GW)]
    p_specs, in_state, out_state, y_specs = _mixer_specs(npc, 1, blocks, layer)
    y_shapes, st_shapes = _mixer_out_shapes(npc, n_sample, blocks)
    bones = _head_block_ones(GROUP, R_HEAD)
    return pl.pallas_call(
        functools.partial(_rwkv_kernel, npc=npc, t_s=t_s),
        grid=(npc + n_sample // 2,),
        in_specs=p_specs + in_state + [_layer_spec(c.shape[1:], layer) for c in consts]
        + [pl.BlockSpec(bones.shape, lambda g: (0, 0))],
        out_specs=y_specs + out_state,
        out_shape=y_shapes + st_shapes,
        scratch_shapes=[pltpu.VMEM((2, R_HEADS, 128, 128), F32), pltpu.VMEM((2, 8, GW), F32)],
        compiler_params=_cparams(("arbitrary",)),
        name="rwkv",
    )(p3, p3, w0s, w0s, sh0, sh0, *consts, bones)


def _moba_prep_kernel(p_ref, cos_ref, sin_ref, qn_ref, kn_ref, bones_ref, q_ref, k_ref, kb_ref, vt_ref, km_ref):
    bones = bones_ref[...]
    cos = jnp.concatenate([cos_ref[...]] * 4, axis=1)
    sin = jnp.concatenate([sin_ref[...]] * 4, axis=1)
    lane = _iota(cos.shape, 1)
    low = (lane % A_HEAD) < (A_ROT // 2)

    def norm_rope(x, w):
        ms = _dot_sel_rhs(x * x, bones) * (1.0 / A_HEAD)
        xn = x * lax.rsqrt(ms + RMS_EPS) * w
        partner = jnp.where(low, pltpu.roll(xn, GROUP - A_ROT // 2, 1), pltpu.roll(xn, A_ROT // 2, 1))
        return xn * cos + partner * sin

    q_ref[...] = norm_rope(p_ref[:, 0:512], qn_ref[...])
    kr = norm_rope(p_ref[:, 512:1024], kn_ref[...])
    k_ref[...] = kr
    kb_ref[...] = kr.astype(BF16)
    vt_ref[...] = p_ref[:, 1024:1536].T.astype(BF16)
    km_ref[...] = jnp.mean(kr, axis=0, keepdims=True)


def _moba_prep(p, cos_t, sin_t, q_norm, k_norm):
    rows = p.shape[0]
    nt = rows // MOBA_BLOCK
    rspec = pl.BlockSpec((MOBA_BLOCK, GROUP), lambda i: (i, 0))
    tspec = pl.BlockSpec((MOBA_BLOCK, LANES), lambda i: (i, 0))
    cspec = pl.BlockSpec((1, GROUP), lambda i: (0, 0))
    return pl.pallas_call(
        _moba_prep_kernel,
        grid=(nt,),
        in_specs=[pl.BlockSpec((MOBA_BLOCK, GW), lambda i: (i, 3)), tspec, tspec, cspec, cspec,
                  pl.BlockSpec((GROUP, GROUP), lambda i: (0, 0))],
        out_specs=[rspec, rspec, rspec, pl.BlockSpec((GROUP, MOBA_BLOCK), lambda i: (0, i)),
                   pl.BlockSpec((None, 1, GROUP), lambda i: (i, 0, 0))],
        out_shape=[jax.ShapeDtypeStruct((rows, GROUP), F32), jax.ShapeDtypeStruct((rows, GROUP), F32),
                   jax.ShapeDtypeStruct((rows, GROUP), BF16), jax.ShapeDtypeStruct((GROUP, rows), BF16),
                   jax.ShapeDtypeStruct((nt, 1, GROUP), F32)],
        compiler_params=_cparams(("parallel",)),
        name="moba_prep",
    )(p, cos_t, sin_t, q_norm, k_norm, _head_block_ones(GROUP, A_HEAD))


def _moba_prompt_kernel(q_ref, kb_ref, vt_ref, km_ref, y_ref, *, nb):
    i = pl.program_id(1)
    blk = MOBA_BLOCK
    scale = A_HEAD ** -0.5
    lane = _iota((blk, LANES), 1)
    lane_km = _iota((nb, LANES), 1)
    bidx = _iota((nb, blk), 0)
    causal = _iota((blk, blk), 0) <= _iota((blk, blk), 1)
    row0 = pl.multiple_of(i * blk, blk)
    prs = range(A_HEADS // 2)
    k_own = [kb_ref[pl.ds(row0, blk), 128 * pr:128 * pr + 128] for pr in prs]
    vt_own = [vt_ref[128 * pr:128 * pr + 128, pl.ds(row0, blk)] for pr in prs]
    hp = [(pr, half) for pr in prs for half in range(2)]
    qbs = [(jnp.where((lane // 64) == half, q_ref[:, 128 * pr:128 * pr + 128], 0.0) * scale).astype(BF16)
           for pr, half in hp]
    gates = [_dot_nt(jnp.where((lane_km // 64) == half, km_ref[:, 128 * pr:128 * pr + 128], 0.0),
                     q_ref[:, 128 * pr:128 * pr + 128], HI) for pr, half in hp]
    s_owns = [_dot_nt(k_own[pr], qb) for (pr, _), qb in zip(hp, qbs)]
    heads, p0s = [], []
    for (pr, _), qb, gate, s_own in zip(hp, qbs, gates, s_owns):
        gate = jnp.where(bidx < i, gate, -jnp.inf)
        picks = []
        for _ in range(MOBA_TOPK):
            mx = jnp.max(gate, axis=0, keepdims=True)
            idx = jnp.min(jnp.where(gate == mx, bidx, nb), axis=0, keepdims=True)
            picks.append(jnp.where(mx > -jnp.inf, idx, -1))
            gate = jnp.where(bidx == idx, -jnp.inf, gate)
        s_own = jnp.where(causal, s_own, NEG)
        m0 = jnp.max(s_own, axis=0, keepdims=True)
        p0 = jnp.exp(s_own - m0)
        p0s.append(p0.astype(BF16))
        heads.append(dict(pair=pr, qb=qb, picks=picks, m0=m0, l0=jnp.sum(p0, axis=0, keepdims=True)))
    acc0s = [_dot(vt_own[hd['pair']], p0) for hd, p0 in zip(heads, p0s)]
    inits = tuple((hd['m0'], hd['l0'], acc0) for hd, acc0 in zip(heads, acc0s))

    def body(j, carry):
        c0 = pl.multiple_of(j * blk, blk)
        kjs = [kb_ref[pl.ds(c0, blk), 128 * pr:128 * pr + 128] for pr in range(A_HEADS // 2)]
        vtjs = [vt_ref[128 * pr:128 * pr + 128, pl.ds(c0, blk)] for pr in range(A_HEADS // 2)]
        scores = [_dot_nt(kjs[hd['pair']], hd['qb']) for hd in heads]
        sts = []
        for hd, (m, l, acc), sc in zip(heads, carry, scores):
            picks = hd['picks']
            sel = jnp.logical_or(jnp.logical_or(picks[0] == j, picks[1] == j), picks[2] == j)
            sj = jnp.where(sel, sc, NEG)
            m_new = jnp.maximum(m, jnp.max(sj, axis=0, keepdims=True))
            alpha = jnp.exp(m - m_new)
            pj = jnp.exp(sj - m_new)
            sts.append((m_new, alpha, alpha * l + jnp.sum(pj, axis=0, keepdims=True), pj.astype(BF16)))
        pvs = [_dot(vtjs[hd['pair']], st[3]) for hd, st in zip(heads, sts)]
        return tuple((st[0], st[2], st[1] * acc + pv) for st, (_, _, acc), pv in zip(sts, carry, pvs))

    final = lax.fori_loop(0, i, body, inits)
    outs = [acc / l for _, l, acc in final]
    for pair in range(A_HEADS // 2):
        both = jnp.concatenate([outs[2 * pair][0:64], outs[2 * pair + 1][64:128]], axis=0)
        y_ref[:, 128 * pair:128 * pair + 128] = both.T


def _moba_prompt(q, kb, vt, km, *, nbatch, t):
    nb = t // MOBA_BLOCK
    return pl.pallas_call(
        functools.partial(_moba_prompt_kernel, nb=nb),
        grid=(nbatch, nb),
        in_specs=[pl.BlockSpec((MOBA_BLOCK, GROUP), lambda b, i: (b * nb + i, 0)),
                  pl.BlockSpec((t, GROUP), lambda b, i: (b, 0)),
                  pl.BlockSpec((GROUP, t), lambda b, i: (0, b)),
                  pl.BlockSpec((None, nb, GROUP), lambda b, i: (b, 0, 0))],
        out_specs=pl.BlockSpec((MOBA_BLOCK, GROUP), lambda b, i: (b * nb + i, 0)),
        out_shape=jax.ShapeDtypeStruct((nbatch * t, GROUP), F32),
        compiler_params=_cparams(("parallel", "arbitrary")),
        name="moba_prompt",
    )(q, kb, vt, km)


def _moba_gate_kernel(pt_ref, *refs, npages_step, nblocks, page):
    del pt_ref
    pg_refs = refs[:npages_step]
    qb_ref, sel_ref, g_sc = refs[npages_step:]
    st = pl.program_id(1)
    per = npages_step // 2
    tq = qb_ref.shape[0]
    for i in range(per):
        part = pg_refs[2 * i][...] + pg_refs[2 * i + 1][...]
        for qi in range(tq):
            prod = part * qb_ref[qi]
            g_sc[qi, st * per + i] = jnp.concatenate(
                [jnp.sum(prod[h], axis=0, keepdims=True) for h in range(A_HEADS)], axis=0)

    @pl.when(st == pl.num_programs(1) - 1)
    def _():
        nidx = _iota((nblocks, A_HEADS, 1), 0)
        for qi in range(tq):
            gate = jnp.sum(g_sc[qi], axis=-1, keepdims=True) * (1.0 / (2 * page))
            for slot in range(MOBA_TOPK):
                mx = jnp.max(gate, axis=0, keepdims=True)
                idx = jnp.min(jnp.where(gate == mx, nidx, nblocks), axis=0, keepdims=True)
                sel_ref[qi * MOBA_TOPK + slot] = jnp.broadcast_to(idx[0], (A_HEADS, LANES))
                gate = jnp.where(nidx == idx, -jnp.inf, gate)


def _moba_gate(cache_kt, layer, page_table, qb):
    nseq, npages = page_table.shape
    page = cache_kt.shape[-1]
    nblocks = npages * page // MOBA_BLOCK
    npages_step = 8
    tq = qb.shape[1]

    def pg_spec(i):
        return pl.BlockSpec((None, None, A_HEADS, A_HEAD, page),
                            lambda b, st, pt: (layer, pt[b, st * npages_step + i], 0, 0, 0))

    return pl.pallas_call(
        functools.partial(_moba_gate_kernel, npages_step=npages_step, nblocks=nblocks, page=page),
        grid_spec=pltpu.PrefetchScalarGridSpec(
            num_scalar_prefetch=1,
            grid=(nseq, npages // npages_step),
            in_specs=[pg_spec(i) for i in range(npages_step)]
            + [pl.BlockSpec((None, tq, A_HEADS, A_HEAD, page), lambda b, st, pt: (b, 0, 0, 0, 0))],
            out_specs=pl.BlockSpec((None, tq * MOBA_TOPK, A_HEADS, LANES), lambda b, st, pt: (b, 0, 0, 0)),
            scratch_shapes=[pltpu.VMEM((tq, nblocks, A_HEADS, page), F32)]),
        out_shape=jax.ShapeDtypeStruct((nseq, tq * MOBA_TOPK, A_HEADS, LANES), jnp.int32),
        compiler_params=_cparams(("parallel", "arbitrary")),
        name="moba_gate",
    )(page_table, *([cache_kt] * npages_step), qb)


def _moba_sample_kernel(pg_ref, q_ref, kn_ref, vn_ref, ck_ref, cv_ref, o_ref, kbuf, vbuf, sem, *, layer, tq, page):
    b, h = pl.program_id(0), pl.program_id(1)
    nslab = tq * MOBA_TOPK * (MOBA_BLOCK // page)
    base = (b * pl.num_programs(1) + h) * nslab

    def copies(c):
        pg = pg_ref[base + c]
        return (pltpu.make_async_copy(ck_ref.at[layer, pg, h], kbuf.at[:, pl.ds(c * page, page)], sem.at[0]),
                pltpu.make_async_copy(cv_ref.at[layer, pg, h], vbuf.at[:, pl.ds(c * page, page)], sem.at[1]))

    for c in range(nslab):
        for cp in copies(c):
            cp.start()
    for c in range(nslab):
        for cp in copies(c):
            cp.wait()
    scale = A_HEAD ** -0.5
    nk = nslab * page
    q = q_ref[...]
    rows = q.shape[0]
    owner = _iota((rows, nk), 1) // (MOBA_TOPK * MOBA_BLOCK)
    s_sel = jnp.where(owner == _iota((rows, nk), 0), _dotb(q, kbuf[...]) * scale, NEG)
    r8, c8 = _iota((rows, rows), 0), _iota((rows, rows), 1)
    s_own = jnp.where(jnp.logical_and(c8 <= r8, c8 < tq), _dotb_nt(q, kn_ref[...]) * scale, NEG)
    m = jnp.maximum(jnp.max(s_sel, axis=1, keepdims=True), jnp.max(s_own, axis=1, keepdims=True))
    p_sel = jnp.exp(s_sel - m)
    p_own = jnp.exp(s_own - m)
    l = jnp.sum(p_sel, axis=1, keepdims=True) + jnp.sum(p_own, axis=1, keepdims=True)
    o_ref[...] = (_dotb_nt(p_sel, vbuf[...]) + _dotb(p_own, vn_ref[...])) / l


def _moba_sample(pages, q_s, kn_s, vn_s, cache_kt, cache_vt, layer, *, tq):
    nseq, nh, rows, hd = q_s.shape
    page = cache_kt.shape[-1]
    nslab = tq * MOBA_TOPK * (MOBA_BLOCK // page)
    spec = pl.BlockSpec((None, None, rows, hd), lambda b, h, pg: (b, h, 0, 0))
    return pl.pallas_call(
        functools.partial(_moba_sample_kernel, layer=layer, tq=tq, page=page),
        grid_spec=pltpu.PrefetchScalarGridSpec(
            num_scalar_prefetch=1,
            grid=(nseq, nh),
            in_specs=[spec, spec, spec, pl.BlockSpec(memory_space=pl.ANY), pl.BlockSpec(memory_space=pl.ANY)],
            out_specs=spec,
            scratch_shapes=[pltpu.VMEM((hd, nslab * page), F32), pltpu.VMEM((hd, nslab * page), F32),
                            pltpu.SemaphoreType.DMA((2,))]),
        out_shape=jax.ShapeDtypeStruct((nseq, nh, rows, hd), F32),
        compiler_params=_cparams(("arbitrary", "arbitrary")),
        name="moba_sample",
    )(pages, q_s, kn_s, vn_s, cache_kt, cache_vt)


def _pack_w_in(w):
    def dz(n):
        return jnp.zeros(w.shape[:2] + (n,), w.dtype)

    o_r, o_g, o_a = P_M, P_M + P_R, P_M + P_R + P_G
    kd = 2 * G_HEADS * G_DK + GROUP
    parts = [w[..., 0:P_M], dz(GW - P_M),
             w[..., o_r:o_r + P_R], dz(GW - P_R),
             w[..., o_g:o_g + kd], w[..., o_g + kd + G_LORA:o_g + P_G], w[..., o_g + kd:o_g + kd + G_LORA],
             dz(GW - P_G),
             w[..., o_a:o_a + P_A], dz(GW - P_A)]
    return jnp.concatenate(parts, axis=-1).astype(BF16)


def _even_head(nheads):
    return (jnp.arange(nheads) % 2 == 0).reshape((nheads, 1, 1))


def _place_half(x, nheads, axis):
    z = jnp.zeros_like(x)
    even = _even_head(nheads)
    return jnp.concatenate([jnp.where(even, x, z), jnp.where(even, z, x)], axis=axis)


def _take_half(x, nheads, axis):
    even = _even_head(nheads)
    lo = lax.slice_in_dim(x, 0, 64, axis=x.ndim + axis)
    hi = lax.slice_in_dim(x, 64, 128, axis=x.ndim + axis)
    return jnp.where(even, lo, hi)


def kernel(x_prompt, x_sample, cache_k, cache_v, state_mlstm_c, state_mlstm_n, state_mlstm_m, state_rwkv, state_rwkv_shift, state_gla, page_table, ffn1_norm, ffn1_w_in, ffn1_w_out, mix_norm, w_in, w_out, ffn2_norm, ffn2_w_in, ffn2_w_out, mlstm_gate_b, mlstm_norm_w, rwkv_mu, rwkv_w0, rwkv_w2, rwkv_a0, rwkv_a2, rwkv_g2, rwkv_k_k, rwkv_k_a, rwkv_r_k, rwkv_ln_w, rwkv_ln_b, gla_gk_up, gla_gk_b, gla_norm_w, moba_q_norm, moba_k_norm):
    nb, t, d = x_prompt.shape
    ns, ts, _ = x_sample.shape
    depth = w_in.shape[0]
    assert nb == 2 and ns % 2 == 0 and t % MOBA_BLOCK == 0 and ts <= 8
    npc = t // CH
    n_prompt = nb * t
    rows = n_prompt + ns * CH
    page = cache_k.shape[2]
    past_len = page_table.shape[1] * page
    cache_kt = jnp.transpose(cache_k, (0, 1, 3, 4, 2))
    cache_vt = jnp.transpose(cache_v, (0, 1, 3, 4, 2))
    kw = dict(npc=npc, n_sample=ns, t_s=ts)
    tm = 512 if rows % 512 == 0 else 256

    x = jnp.concatenate([x_prompt.reshape(n_prompt, d),
                         jnp.pad(x_sample, ((0, 0), (0, CH - ts), (0, 0))).reshape(ns * CH, d)], axis=0)
    pos = jnp.concatenate([jnp.tile(jnp.arange(t), nb), jnp.tile(past_len + jnp.arange(CH), ns)])
    inv = ROPE_THETA ** (-(jnp.arange(A_ROT // 2, dtype=F32) * 2.0 / A_ROT))
    ang = pos.astype(F32)[:, None] * inv[None, :]
    one = jnp.ones((rows, A_HEAD - A_ROT), F32)
    cos_h = jnp.concatenate([jnp.cos(ang), jnp.cos(ang), one], axis=1)
    sin_h = jnp.concatenate([-jnp.sin(ang), jnp.sin(ang), 0.0 * one], axis=1)
    cos_t = jnp.concatenate([cos_h, cos_h], axis=1)
    sin_t = jnp.concatenate([sin_h, sin_h], axis=1)

    w_in_p = _pack_w_in(w_in)
    w_out_b = w_out.astype(BF16)
    f1_in, f1_out = ffn1_w_in.astype(BF16), ffn1_w_out.astype(BF16)
    f2_in, f2_out = ffn2_w_in.astype(BF16), ffn2_w_out.astype(BF16)

    def with_prompt_zeros(st):
        return jnp.concatenate([jnp.zeros((depth, nb) + st.shape[2:], F32), st], axis=1)

    def sample_rows(a, n):
        return a[n_prompt:].reshape(ns, CH, a.shape[-1])[:, :n]

    def heads_major(a):
        return a.reshape(ns, 8, A_HEADS, A_HEAD).transpose(0, 2, 1, 3)

    ce = jnp.concatenate([state_mlstm_c, state_mlstm_n[..., None],
                          jnp.zeros(state_mlstm_n.shape + (LANES - 1,), F32)], axis=-1)
    ce0 = with_prompt_zeros(_place_half(ce, M_HEADS, -2))
    m0 = jnp.zeros((depth, nb + ns, 8, LANES), F32).at[:, nb:, 0, :M_HEADS].set(state_mlstm_m)
    gate_b = jnp.zeros((depth, 1, LANES), F32).at[:, 0, :2 * M_HEADS].set(mlstm_gate_b)
    w0s = with_prompt_zeros(_place_half(_place_half(jnp.swapaxes(state_rwkv, -1, -2), R_HEADS, -2), R_HEADS, -1))
    sh0 = jnp.zeros((depth, nb + ns, 8, GW), F32).at[:, nb:, :, :P_R].set(state_rwkv_shift[:, :, None, :])
    r_consts = _rwkv_consts(rwkv_mu, rwkv_w0, rwkv_w2, rwkv_a0, rwkv_a2, rwkv_g2, rwkv_k_k, rwkv_k_a,
                            rwkv_r_k.reshape(depth, GROUP), rwkv_ln_w, rwkv_ln_b)
    s0 = with_prompt_zeros(_place_half(state_gla, G_HEADS, -2))
    gk_up = jnp.zeros((depth, LANES, G_HEADS * G_DK), F32).at[:, :G_LORA].set(gla_gk_up)
    q_norm = jnp.tile(moba_q_norm, (1, A_HEADS)).reshape(depth, 1, GROUP)
    k_norm = jnp.tile(moba_k_norm, (1, A_HEADS)).reshape(depth, 1, GROUP)
    nbk = t // MOBA_BLOCK
    per_blk = MOBA_BLOCK // page

    raw = []
    for l in range(depth):
        x = _ffn(x, ffn1_norm[l], f1_in, f1_out, l, tm=tm, n_prompt=n_prompt)
        p = _proj(x, mix_norm[l], w_in_p, l, tm=tm, n_prompt=n_prompt)
        p3 = p.reshape(rows // CH, CH, NPK)
        ym_p, ym_s, ce_a, ce_b, m_a, m_b = _mlstm(p3, ce0, m0, gate_b, mlstm_norm_w.reshape(depth, 1, GROUP), l, **kw)
        yr_p, yr_s, w_a, w_b, sh_a, sh_b = _rwkv(p3, w0s, sh0, r_consts, l, **kw)
        yg_p, yg_s, g_a, g_b = _gla(p3, s0, gk_up, gla_gk_b.reshape(depth, 1, -1),
                                    gla_norm_w.reshape(depth, 1, G_DV), l, **kw)

        q_all, k_all, kb, vt, km = _moba_prep(p, cos_t, sin_t, q_norm[l], k_norm[l])
        v_all = p[:, 3 * GW + 2 * GROUP:3 * GW + 3 * GROUP]
        ya_p = _moba_prompt(q_all, kb, vt, km[:nb * nbk].reshape(nb, nbk, GROUP), nbatch=nb, t=t)
        q_s = sample_rows(q_all, 8)
        q_lanes = jnp.broadcast_to(q_s[:, :ts].reshape(ns, ts, A_HEADS, A_HEAD, 1), (ns, ts, A_HEADS, A_HEAD, page))
        sel = _moba_gate(cache_kt, l, page_table, q_lanes)[..., 0]
        sel = sel.reshape(ns, ts, MOBA_TOPK, A_HEADS).transpose(0, 3, 1, 2)
        pidx = per_blk * sel[..., None] + jnp.arange(per_blk)
        pages = page_table[jnp.arange(ns).reshape(ns, 1, 1, 1, 1), pidx].reshape(-1).astype(jnp.int32)
        o_s = _moba_sample(pages, heads_major(q_s), heads_major(sample_rows(k_all, 8)),
                           heads_major(sample_rows(v_all, 8)), cache_kt, cache_vt, l, tq=ts)
        ya_s = jnp.pad(o_s.transpose(0, 2, 1, 3).reshape(ns, 8, GROUP), ((0, 0), (0, CH - 8), (0, 0)))

        x = _outproj(x, [y.reshape(n_prompt, GROUP) for y in (ym_p, yr_p, yg_p, ya_p)],
                     [y.reshape(ns * CH, GROUP) for y in (ym_s, yr_s, yg_s, ya_s)], w_out_b, l, tm=tm)
        x = _ffn(x, ffn2_norm[l], f2_in, f2_out, l, tm=tm, n_prompt=n_prompt)
        raw.append((k_all, v_all, ce_a, ce_b, m_a, m_b, w_a, w_b, sh_a, sh_b, g_a, g_b))

    k_all, v_all, ce_a, ce_b, m_a, m_b, w_a, w_b, sh_a, sh_b, g_a, g_b = (
        jnp.stack([lay[i] for lay in raw]) for i in range(len(raw[0])))
    shp_p, shp_s = (depth, nb, t, A_HEADS, A_HEAD), (depth, ns, ts, A_HEADS, A_HEAD)

    def kv_sample(a):
        return a[:, n_prompt:].reshape(depth, ns, CH, GROUP)[:, :, :ts].reshape(shp_s)

    ce_p, ce_s = (_take_half(c, M_HEADS, -2) for c in _merge_state(ce_a, ce_b, nb))
    mm_p, mm_s = (m[:, :, 0, :M_HEADS] for m in _merge_state(m_a, m_b, nb))
    rw_p, rw_s = (jnp.swapaxes(_take_half(_take_half(w, R_HEADS, -2), R_HEADS, -1), -1, -2)
                  for w in _merge_state(w_a, w_b, nb))
    sh_p, sh_s = (sh[:, :, 0, :P_R] for sh in _merge_state(sh_a, sh_b, nb))
    gl_p, gl_s = (_take_half(gs, G_HEADS, -2) for gs in _merge_state(g_a, g_b, nb))
    y_p = x[:n_prompt].reshape(nb, t, d)
    y_s = x[n_prompt:].reshape(ns, CH, d)[:, :ts]
    return (y_p, y_s, k_all[:, :n_prompt].reshape(shp_p), v_all[:, :n_prompt].reshape(shp_p),
            kv_sample(k_all), kv_sample(v_all),
            ce_p[..., :M_DV], ce_s[..., :M_DV], ce_p[..., M_DV], ce_s[..., M_DV], mm_p, mm_s,
            rw_p, rw_s, sh_p, sh_s, gl_p, gl_s)
```

```python
import functools

import jax
import jax.numpy as jnp
import numpy as np
from jax import lax
from jax.experimental import pallas as pl
from jax.experimental.pallas import tpu as pltpu

F32 = jnp.float32
BF16 = jnp.bfloat16
HI = lax.Precision.HIGHEST

D_MODEL = 2048
GROUP = D_MODEL // 4
M_HEADS = 4
M_DV = GROUP // M_HEADS
M_DK = M_DV // 2
GATE_CAP = 15.0
R_HEAD = 64
R_HEADS = GROUP // R_HEAD
R_LORA = 32
R_GN_EPS = 64e-5
G_HEADS = 4
G_DV = GROUP // G_HEADS
G_DK = G_DV // 2
G_LORA = 16
G_TAU = 16.0
A_HEAD = 64
A_HEADS = GROUP // A_HEAD
A_ROT = A_HEAD // 4
ROPE_THETA = 500000.0
MOBA_BLOCK = 256
MOBA_TOPK = 3
RMS_EPS = 1e-6
P_M = 2 * M_HEADS * M_DK + 2 * GROUP + 2 * M_HEADS
P_R = 3 * GROUP + 3 * R_LORA
P_G = 2 * G_HEADS * G_DK + 2 * GROUP + G_LORA
P_A = 3 * GROUP

CH = 64
GW = 1664
NPK = 4 * GW
NEG = -1e30
LANES = 128
VMEM_LIMIT = 48 * 1024 * 1024


def _dot(a, b, prec=None):
    return jnp.dot(a, b, preferred_element_type=F32, precision=prec)


def _dot_nt(a, b, prec=None):
    return lax.dot_general(a, b, (((1,), (1,)), ((), ())), preferred_element_type=F32, precision=prec)


def _dot_tn(a, b, prec=None):
    return lax.dot_general(a, b, (((0,), (0,)), ((), ())), preferred_element_type=F32, precision=prec)


def _bf(x):
    return x.astype(BF16)


def _dotb(a, b):
    return _dot(_bf(a), _bf(b))


def _dotb_nt(a, b):
    return _dot_nt(_bf(a), _bf(b))


def _dotb_tn(a, b):
    return _dot_tn(_bf(a), _bf(b))


def _split2(x):
    hi = x.astype(BF16)
    return hi, (x - hi.astype(F32)).astype(BF16)


def _dot_sel_rhs(a, sel):
    hi, lo = _split2(a)
    sel = _bf(sel)
    return _dot(hi, sel) + _dot(lo, sel)


def _dot_sel_lhs(sel, b):
    hi, lo = _split2(b)
    sel = _bf(sel)
    return _dot(sel, hi) + _dot(sel, lo)


def _row_to_col(row, eye):
    return jnp.sum(eye * row, axis=1, keepdims=True)


def _iota(shape, dim):
    return lax.broadcasted_iota(jnp.int32, shape, dim)


def _sigmoid(x):
    return 1.0 / (1.0 + jnp.exp(-x))


def _softplus(x):
    return jnp.maximum(x, 0.0) + jnp.log(1.0 + jnp.exp(-jnp.abs(x)))


def _log_sigmoid(x):
    return -_softplus(-x)


def _tril(n):
    return (_iota((n, n), 1) <= _iota((n, n), 0)).astype(F32)


def _cparams(sem, vmem=VMEM_LIMIT):
    return pltpu.CompilerParams(dimension_semantics=sem, vmem_limit_bytes=vmem)


def _rms_rows(x, w):
    return x * lax.rsqrt(jnp.mean(x * x, axis=-1, keepdims=True) + RMS_EPS) * w


def _sample_rows(ref, rows_per_slot):
    tm, c = ref.shape
    return ref[...].reshape(tm // CH, CH, c)[:, 0:rows_per_slot, :].reshape(tm // CH * rows_per_slot, c)


def _scatter_sample_rows(o_ref, vals, rows_per_slot):
    for b in range(o_ref.shape[0] // CH):
        o_ref[CH * b:CH * b + rows_per_slot, :] = vals[rows_per_slot * b:rows_per_slot * (b + 1), :]


def _ffn_kernel(x_ref, nw_ref, wg_ref, wu_ref, wo_ref, o_ref, h_ref, acc_ref, *, prompt_tiles, ms):
    i, j = pl.program_id(0), pl.program_id(1)
    last = j == pl.num_programs(1) - 1

    def swiglu_part(h):
        gate = _dot(h, wg_ref[...])
        up = _dot(h, wu_ref[...])
        return _dot((gate * _sigmoid(gate) * up).astype(BF16), wo_ref[...])

    @pl.when(i < prompt_tiles)
    def _():
        @pl.when(j == 0)
        def _():
            h_ref[...] = _rms_rows(x_ref[...], nw_ref[...]).astype(BF16)
            acc_ref[...] = jnp.zeros_like(acc_ref)

        acc_ref[...] += swiglu_part(h_ref[...])

        @pl.when(last)
        def _():
            o_ref[...] = x_ref[...] + 0.5 * acc_ref[...]

    @pl.when(i >= prompt_tiles)
    def _():
        @pl.when(j == 0)
        def _():
            h_ref[0:ms, :] = _rms_rows(_sample_rows(x_ref, 8), nw_ref[...]).astype(BF16)
            acc_ref[0:ms, :] = jnp.zeros((ms, acc_ref.shape[1]), F32)

        acc_ref[0:ms, :] += swiglu_part(h_ref[0:ms, :])

        @pl.when(last)
        def _():
            o_ref[...] = x_ref[...]
            _scatter_sample_rows(o_ref, _sample_rows(x_ref, 8) + 0.5 * acc_ref[0:ms, :], 8)


def _ffn(x, norm_w, w_in, w_out, layer, *, tm, n_prompt, tf=512):
    rows, d = x.shape
    dff = w_out.shape[1]
    nj = dff // tf
    return pl.pallas_call(
        functools.partial(_ffn_kernel, prompt_tiles=n_prompt // tm, ms=tm // CH * 8),
        grid=(rows // tm, nj),
        in_specs=[
            pl.BlockSpec((tm, d), lambda i, j: (i, 0)),
            pl.BlockSpec((1, d), lambda i, j: (0, 0)),
            pl.BlockSpec((None, d, tf), lambda i, j: (layer, 0, j)),
            pl.BlockSpec((None, d, tf), lambda i, j: (layer, 0, j + nj)),
            pl.BlockSpec((None, tf, d), lambda i, j: (layer, j, 0)),
        ],
        out_specs=pl.BlockSpec((tm, d), lambda i, j: (i, 0)),
        out_shape=jax.ShapeDtypeStruct((rows, d), F32),
        scratch_shapes=[pltpu.VMEM((tm, d), BF16), pltpu.VMEM((tm, d), F32)],
        compiler_params=_cparams(("parallel", "arbitrary")),
        name="ffn",
    )(x, norm_w.reshape(1, d), w_in, w_in, w_out)


def _proj_kernel(x_ref, nw_ref, w_ref, o_ref, h_ref, *, prompt_tiles, ms):
    i, j = pl.program_id(0), pl.program_id(1)

    @pl.when(i < prompt_tiles)
    def _():
        @pl.when(j == 0)
        def _():
            h_ref[...] = _rms_rows(x_ref[...], nw_ref[...]).astype(BF16)

        o_ref[...] = _dot(h_ref[...], w_ref[...])

    @pl.when(i >= prompt_tiles)
    def _():
        @pl.when(j == 0)
        def _():
            h_ref[0:ms, :] = _rms_rows(_sample_rows(x_ref, 8), nw_ref[...]).astype(BF16)

        o_ref[...] = jnp.zeros_like(o_ref)
        _scatter_sample_rows(o_ref, _dot(h_ref[0:ms, :], w_ref[...]), 8)


def _proj(x, norm_w, w, layer, *, tm, n_prompt, tn=GW):
    rows, d = x.shape
    n = w.shape[2]
    return pl.pallas_call(
        functools.partial(_proj_kernel, prompt_tiles=n_prompt // tm, ms=tm // CH * 8),
        grid=(rows // tm, n // tn),
        in_specs=[
            pl.BlockSpec((tm, d), lambda i, j: (i, 0)),
            pl.BlockSpec((1, d), lambda i, j: (0, 0)),
            pl.BlockSpec((None, d, tn), lambda i, j: (layer, 0, j)),
        ],
        out_specs=pl.BlockSpec((tm, tn), lambda i, j: (i, j)),
        out_shape=jax.ShapeDtypeStruct((rows, n), F32),
        scratch_shapes=[pltpu.VMEM((tm, d), BF16)],
        compiler_params=_cparams(("parallel", "arbitrary")),
        name="proj",
    )(x, norm_w.reshape(1, d), w)


def _outproj_kernel(x_ref, *refs, prompt_tiles):
    yp_refs, ys_refs, w_ref, o_ref = refs[0:4], refs[4:8], refs[8], refs[9]

    def run(y_refs):
        acc = x_ref[...]
        for gi, y_ref in enumerate(y_refs):
            acc = acc + _dot(y_ref[...].astype(BF16), w_ref[gi * GROUP:(gi + 1) * GROUP, :])
        o_ref[...] = acc

    @pl.when(pl.program_id(0) < prompt_tiles)
    def _():
        run(yp_refs)

    @pl.when(pl.program_id(0) >= prompt_tiles)
    def _():
        run(ys_refs)


def _outproj(x, y_prompt, y_sample, w, layer, *, tm):
    rows, d = x.shape
    pt = y_prompt[0].shape[0] // tm
    p_spec = pl.BlockSpec((tm, GROUP), lambda i: (jnp.minimum(i, pt - 1), 0))
    s_spec = pl.BlockSpec((tm, GROUP), lambda i: (jnp.maximum(i - pt, 0), 0))
    return pl.pallas_call(
        functools.partial(_outproj_kernel, prompt_tiles=pt),
        grid=(rows // tm,),
        in_specs=[pl.BlockSpec((tm, d), lambda i: (i, 0))] + [p_spec] * 4 + [s_spec] * 4
        + [pl.BlockSpec((None, d, d), lambda i: (layer, 0, 0))],
        out_specs=pl.BlockSpec((tm, d), lambda i: (i, 0)),
        out_shape=jax.ShapeDtypeStruct((rows, d), F32),
        compiler_params=_cparams(("parallel",)),
        name="outproj",
    )(x, *y_prompt, *y_sample, w)


def _layer_spec(shape, layer):
    return pl.BlockSpec((None,) + shape, lambda g: (layer,) + (0,) * len(shape))


def _mixer_specs(npc, group, state_blocks, layer):
    def chunk_idx(s):
        return lambda g: (jnp.where(g < npc, s * npc + g, 2 * npc + 2 * (g - npc) + s), 0, group)

    p_specs = [pl.BlockSpec((None, CH, GW), chunk_idx(s)) for s in (0, 1)]
    in_state, out_state = [], []
    for blk in state_blocks:
        zeros = (0,) * len(blk)
        for s in (0, 1):
            in_state.append(pl.BlockSpec(
                (None, None) + blk,
                lambda g, s=s, z=zeros: (layer, jnp.where(g < npc, s, 2 + 2 * (g - npc) + s)) + z))
            out_state.append(pl.BlockSpec(
                (None,) + blk, lambda g, z=zeros: (jnp.where(g < npc, 0, 1 + g - npc),) + z))
    y_specs = [pl.BlockSpec((2, None, CH, GROUP), lambda g: (0, jnp.minimum(g, npc - 1), 0, 0)),
               pl.BlockSpec((None, 2, CH, GROUP), lambda g: (jnp.maximum(g - npc, 0), 0, 0, 0))]
    return p_specs, in_state, out_state, y_specs


def _mixer_out_shapes(npc, n_sample, state_blocks):
    ys = [jax.ShapeDtypeStruct((2, npc, CH, GROUP), F32), jax.ShapeDtypeStruct((n_sample // 2, 2, CH, GROUP), F32)]
    st = []
    for blk in state_blocks:
        st += [jax.ShapeDtypeStruct((1 + n_sample // 2,) + blk, F32)] * 2
    return ys, st


def _step_info(npc, t_s):
    g = pl.program_id(0)
    first = jnp.logical_or(g == 0, g >= npc)
    last = g >= npc - 1
    tlen = jnp.where(g < npc, CH, t_s)
    return g, first, last, tlen


def _write_y(g, npc, yp_ref, ys_ref, s, val):
    @pl.when(g < npc)
    def _():
        yp_ref[s] = val

    @pl.when(g >= npc)
    def _():
        ys_ref[s] = val


def _merge_state(a, b, nb):
    assert nb == 2
    prompt = jnp.stack([a[:, 0], b[:, 0]], axis=1)
    sample = jnp.stack([a[:, 1:], b[:, 1:]], axis=2).reshape((a.shape[0], -1) + a.shape[2:])
    return prompt, sample


def _mlstm_kernel(pa_ref, pb_ref, cea_ref, ceb_ref, ma_ref, mb_ref, gb_ref, nw_ref,
                  yp_ref, ys_ref, ceoa_ref, ceob_ref, moa_ref, mob_ref, ce_sc, m_sc, *, npc, t_s):
    g, first, last, tlen = _step_info(npc, t_s)
    slots = ((pa_ref, cea_ref, ma_ref, ceoa_ref, moa_ref), (pb_ref, ceb_ref, mb_ref, ceob_ref, mob_ref))
    row = _iota((CH, LANES), 0)
    lane = _iota((CH, LANES), 1)
    tril = _tril(CH)
    causal = _iota((CH, CH), 1) <= _iota((CH, CH), 0)
    e0 = (lane == 0).astype(F32)
    valid = row < tlen

    @pl.when(first)
    def _():
        for s, (_, ce0_ref, m0_ref, _, _) in enumerate(slots):
            ce_sc[s] = ce0_ref[...]
            m_sc[s] = m0_ref[...]

    units = []
    for s, (p_ref, _, _, _, _) in enumerate(slots):
        gates = p_ref[:, 1536:1664] + gb_ref[...]
        gates = GATE_CAP * jnp.tanh(gates / GATE_CAP)
        ig = jnp.where(valid, gates, NEG)
        lf = jnp.where(valid, _log_sigmoid(gates), 0.0)
        b_col = _dot_sel_lhs(tril, lf)
        ig_t = ig.T
        lf_hi, lf_lo = _split2(lf.T[0:8])
        b_row = _dot_nt(lf_hi, _bf(tril)) + _dot_nt(lf_lo, _bf(tril))
        for h in range(M_HEADS):
            pair, half = h // 2, h % 2
            hm = (lane // 64) == half
            b_c = b_col[:, 4 + h:5 + h]
            m_h = m_sc[s, 0:1, h:h + 1]
            dmat = jnp.where(causal, b_c - b_row[4 + h:5 + h, :] + ig_t[h:h + 1, :], NEG)
            m_inter = b_c + m_h
            mt = jnp.maximum(m_inter, jnp.max(dmat, axis=1, keepdims=True))
            m_new = mt[CH - 1:CH, :]
            b_last = b_c[CH - 1:CH, :]
            units.append(dict(
                s=s, h=h, p_ref=p_ref, mt=mt, m_new=m_new, dexp=jnp.exp(dmat - mt), s_inter=jnp.exp(m_inter - mt),
                carry=jnp.exp(b_last + m_h - m_new), ws=jnp.exp(b_last - b_c + ig[:, h:h + 1] - m_new),
                qm=jnp.where(hm, p_ref[:, 128 * pair:128 * pair + 128], 0.0),
                km=jnp.where(hm, p_ref[:, 256 + 128 * pair:256 + 128 * pair + 128], 0.0) * (M_DK ** -0.5),
                v_ext=jnp.concatenate([p_ref[:, 512 + 128 * h:512 + 128 * h + 128], e0], axis=1)))
    ces = [ce_sc[u['s'], u['h']] for u in units]
    qks = [_dotb_nt(u['qm'], u['km']) for u in units]
    qcs = [_dotb(u['qm'], ce) for u, ce in zip(units, ces)]
    nds = [u['s_inter'] * qc + _dotb(u['dexp'] * qk, u['v_ext']) for u, qk, qc in zip(units, qks, qcs)]
    for u, ce in zip(units, ces):
        ce_sc[u['s'], u['h']] = u['carry'] * ce + _dotb_tn(u['km'], u['ws'] * u['v_ext'])
        m_sc[u['s'], 0:1, u['h']:u['h'] + 1] = u['m_new']
    for s, (p_ref, _, _, ceo_ref, mo_ref) in enumerate(slots):
        ys = []
        for h in range(M_HEADS):
            u, nd = units[s * M_HEADS + h], nds[s * M_HEADS + h]
            hh = nd[:, 0:128] / jnp.maximum(jnp.abs(nd[:, 128:129]), jnp.exp(-u['mt']))
            hn = hh * lax.rsqrt(jnp.mean(hh * hh, axis=-1, keepdims=True) + RMS_EPS)
            hn = hn * nw_ref[:, 128 * h:128 * h + 128]
            ys.append(hn * _sigmoid(p_ref[:, 1024 + 128 * h:1024 + 128 * h + 128]))
        _write_y(g, npc, yp_ref, ys_ref, s, jnp.concatenate(ys, axis=1))

        @pl.when(last)
        def _():
            ceo_ref[...] = ce_sc[s]
            mo_ref[...] = m_sc[s]


def _mlstm(p3, ce0, m0, gate_b, norm_w, layer, *, npc, n_sample, t_s):
    blocks = [(M_HEADS, 128, 256), (8, LANES)]
    p_specs, in_state, out_state, y_specs = _mixer_specs(npc, 0, blocks, layer)
    y_shapes, st_shapes = _mixer_out_shapes(npc, n_sample, blocks)
    return pl.pallas_call(
        functools.partial(_mlstm_kernel, npc=npc, t_s=t_s),
        grid=(npc + n_sample // 2,),
        in_specs=p_specs + in_state + [_layer_spec((1, LANES), layer), _layer_spec((1, GROUP), layer)],
        out_specs=y_specs + out_state,
        out_shape=y_shapes + st_shapes,
        scratch_shapes=[pltpu.VMEM((2, M_HEADS, 128, 256), F32), pltpu.VMEM((2, 8, LANES), F32)],
        compiler_params=_cparams(("arbitrary",)),
        name="mlstm",
    )(p3, p3, ce0, ce0, m0, m0, gate_b, norm_w)


def _gla_kernel(pa_ref, pb_ref, sa_ref, sb_ref, up_ref, gkb_ref, nw_ref,
                yp_ref, ys_ref, soa_ref, sob_ref, s_sc, k_sc, b_sc, *, npc, t_s):
    g, first, last, tlen = _step_info(npc, t_s)
    slots = ((pa_ref, sa_ref, soa_ref), (pb_ref, sb_ref, sob_ref))
    row = _iota((CH, LANES), 0)
    lane = _iota((CH, LANES), 1)
    tril = _tril(CH)
    r128 = _iota((LANES, LANES), 0)
    c128 = _iota((LANES, LANES), 1)
    half_ones = ((r128 // 64) == (c128 // 64)).astype(F32)
    eye = (r128 == c128).astype(F32)
    valid = row < tlen

    @pl.when(first)
    def _():
        for s, (_, s0_ref, _) in enumerate(slots):
            s_sc[s] = s0_ref[...]

    pairs = []
    for s, (p_ref, _, _) in enumerate(slots):
        z = _dotb(p_ref[:, 1536:1664], up_ref[...]) + gkb_ref[...]
        la = _log_sigmoid(z) / G_TAU
        la = jnp.where(jnp.concatenate([valid, valid], axis=1), la, 0.0)
        b = _dot_sel_lhs(tril, la)
        for pair in range(2):
            sl = slice(128 * pair, 128 * pair + 128)
            kp = jnp.where(valid, p_ref[:, 256 + 128 * pair:256 + 128 * pair + 128], 0.0)
            bp = b[:, sl]
            k_sc[s, pair] = kp
            b_sc[s, pair] = bp
            pairs.append(dict(s=s, pair=pair, p_ref=p_ref, qp=p_ref[:, sl] * (G_DK ** -0.5), kp=kp, bp=bp,
                              bl=bp[CH - 1:CH, :]))

    def body(ti, accs):
        r0 = pl.multiple_of(ti * 8, 8)
        sums = []
        for u in pairs:
            k8 = k_sc[u['s'], u['pair'], pl.ds(r0, 8), :]
            b8 = b_sc[u['s'], u['pair'], pl.ds(r0, 8), :]
            e = [u['qp'] * k8[j:j + 1, :] * jnp.exp(jnp.minimum(u['bp'] - b8[j:j + 1, :], 0.0)) for j in range(8)]
            sums.append(_dotb(jnp.concatenate(e, axis=0), half_ones))
        out = []
        for acc, sm in zip(accs, sums):
            for j in range(8):
                acc = jnp.where((lane % 64) == r0 + j, sm[CH * j:CH * (j + 1)], acc)
            out.append(acc)
        return tuple(out)

    atts = lax.fori_loop(0, CH // 8, body, tuple(jnp.zeros((CH, LANES), F32) for _ in pairs))
    units = []
    for u, att in zip(pairs, atts):
        att = jnp.where((lane % 64) <= row, att, 0.0)
        decay = _row_to_col(jnp.exp(u['bl']), eye)
        for half in range(2):
            h = 2 * u['pair'] + half
            hm = (lane // 64) == half
            v_h = u['p_ref'][:, 512 + 128 * h:512 + 128 * h + 128]
            units.append(dict(s=u['s'], h=h, v_h=v_h, decay=decay, p_ref=u['p_ref'],
                              qe=jnp.where(hm, u['qp'] * jnp.exp(u['bp']), 0.0), att=jnp.where(hm, att, 0.0),
                              khat=jnp.where(hm, u['kp'] * jnp.exp(u['bl'] - u['bp']), 0.0)))
    sts = [s_sc[u['s'], u['h']] for u in units]
    outs = [_dotb(u['qe'], st) + _dotb(u['att'], jnp.concatenate([u['v_h'], u['v_h']], axis=0))
            for u, st in zip(units, sts)]
    for u, st in zip(units, sts):
        s_sc[u['s'], u['h']] = u['decay'] * st + _dotb_tn(u['khat'], u['v_h'])
    for s, (p_ref, _, so_ref) in enumerate(slots):
        ys = []
        for h in range(G_HEADS):
            o = outs[s * G_HEADS + h]
            on = o * lax.rsqrt(jnp.mean(o * o, axis=-1, keepdims=True) + RMS_EPS) * nw_ref[...]
            gt = p_ref[:, 1024 + 128 * h:1024 + 128 * h + 128]
            ys.append(on * gt * _sigmoid(gt))
        _write_y(g, npc, yp_ref, ys_ref, s, jnp.concatenate(ys, axis=1))

        @pl.when(last)
        def _():
            so_ref[...] = s_sc[s]


def _gla(p3, s0, gk_up, gk_b, norm_w, layer, *, npc, n_sample, t_s):
    blocks = [(G_HEADS, 128, 128)]
    p_specs, in_state, out_state, y_specs = _mixer_specs(npc, 2, blocks, layer)
    y_shapes, st_shapes = _mixer_out_shapes(npc, n_sample, blocks)
    return pl.pallas_call(
        functools.partial(_gla_kernel, npc=npc, t_s=t_s),
        grid=(npc + n_sample // 2,),
        in_specs=p_specs + in_state
        + [_layer_spec((LANES, 256), layer), _layer_spec((1, 256), layer), _layer_spec((1, G_DV), layer)],
        out_specs=y_specs + out_state,
        out_shape=y_shapes + st_shapes,
        scratch_shapes=[pltpu.VMEM((2, G_HEADS, 128, 128), F32), pltpu.VMEM((2, 2, CH, LANES), F32),
                        pltpu.VMEM((2, 2, CH, LANES), F32)],
        compiler_params=_cparams(("arbitrary",)),
        name="gla",
    )(p3, p3, s0, s0, gk_up, gk_b, norm_w)


def _rwkv_kernel(pa_ref, pb_ref, wa_ref, wb_ref, sha_ref, shb_ref, mu_ref, w0_ref, w2_ref, a0_ref, a2_ref,
                 g2_ref, kk_ref, ka_ref, rk_ref, lnw_ref, lnb_ref, bones_ref,
                 yp_ref, ys_ref, woa_ref, wob_ref, shoa_ref, shob_ref, w_sc, sh_sc, *, npc, t_s):
    g, first, last, tlen = _step_info(npc, t_s)
    slots = ((pa_ref, wa_ref, sha_ref, woa_ref, shoa_ref), (pb_ref, wb_ref, shb_ref, wob_ref, shob_ref))
    lane = _iota((CH, LANES), 1)
    tril = _tril(CH)
    r64, c64 = _iota((CH, CH), 0), _iota((CH, CH), 1)
    lower_strict = c64 < r64
    lower = c64 <= r64
    eye64 = (r64 == c64).astype(F32)
    eye128 = (_iota((LANES, LANES), 0) == _iota((LANES, LANES), 1)).astype(F32)
    bones = bones_ref[...]
    valid = _iota((CH, GROUP), 0) < tlen

    @pl.when(first)
    def _():
        for s, (_, w0s_ref, sh0_ref, _, _) in enumerate(slots):
            w_sc[s] = w0s_ref[...]
            sh_sc[s] = sh0_ref[...]

    units, tails = [], []
    for s, (p_ref, _, _, _, _) in enumerate(slots):
        pf = p_ref[...]
        prev = jnp.where(_iota((CH, GW), 0) == 0, sh_sc[s, 0:1, :], pltpu.roll(pf, 1, 0))
        xs = pf + (prev - pf) * mu_ref[...]
        r = xs[:, 0:512]
        k = xs[:, 512:1024]
        v = xs[:, 1024:1536]
        lo = xs[:, 1536:1664]
        wraw = -_softplus(-(w0_ref[...] + _dotb(jnp.tanh(lo), w2_ref[...]))) - 0.5
        logw = jnp.where(valid, -jnp.exp(wraw), 0.0)
        a = _sigmoid(a0_ref[...] + _dotb(lo, a2_ref[...]))
        gg = _dotb(_sigmoid(lo), g2_ref[...])
        kk = k * kk_ref[...]
        kk = kk / jnp.maximum(jnp.sqrt(_dot_sel_rhs(kk * kk, bones)), 1e-12)
        k2 = k * (1.0 + (a - 1.0) * ka_ref[...])
        k2m = jnp.where(valid, k2, 0.0)
        bv = jnp.where(valid, kk * a, 0.0)
        lg = _dot_sel_lhs(tril, logw)
        lgl = lg[CH - 1:CH, :]
        e_out = jnp.exp(-lg)
        e_end = jnp.exp(lgl - lg)
        at = -kk * jnp.exp(lg - logw)
        rt = r * jnp.exp(lg)
        bt = bv * e_out
        kt = k2m * e_out
        bh = bv * e_end
        kh = k2m * e_end
        gl = jnp.exp(lgl)
        tails.append((r, k2, v, gg))
        for pair in range(R_HEADS // 2):
            sl = slice(128 * pair, 128 * pair + 128)
            decay = _row_to_col(gl[:, sl], eye128)
            for half in range(2):
                hm = (lane // 64) == half
                units.append(dict(
                    s=s, h=2 * pair + half, vp=v[:, sl], decay=decay, bt=bt[:, sl], kt=kt[:, sl],
                    ar=jnp.concatenate([jnp.where(hm, at[:, sl], 0.0), jnp.where(hm, rt[:, sl], 0.0)], axis=0),
                    lhs=jnp.concatenate([jnp.where(hm, bh[:, sl], 0.0), jnp.where(hm, kh[:, sl], 0.0)], axis=0)))
    gbs = [_dotb_nt(u['ar'], u['bt']) for u in units]
    gks = [_dotb_nt(u['ar'], u['kt']) for u in units]
    nmat = [jnp.where(lower_strict, gb[0:CH], 0.0) for gb in gbs]
    tms = [eye64 + n for n in nmat]
    mxs = [_dotb(n, n) for n in nmat]
    for it in range(5):
        prods = [_dotb(jnp.concatenate([tm, mx], axis=0), mx) for tm, mx in zip(tms, mxs)]
        tms = [tm + pr[0:CH] for tm, pr in zip(tms, prods)]
        if it < 4:
            mxs = [pr[CH:2 * CH] for pr in prods]
    wsts = [w_sc[u['s'], u['h']] for u in units]
    arws = [_dotb(u['ar'], wst) for u, wst in zip(units, wsts)]
    inner = [_dotb(jnp.where(lower_strict, gk[0:CH], 0.0), u['vp']) for u, gk in zip(units, gks)]
    pms = [_dotb(tm, arw[0:CH] + inn) for tm, arw, inn in zip(tms, arws, inner)]
    yhs = [arw[CH:2 * CH] + _dotb(jnp.where(lower, gb[CH:2 * CH], 0.0), pm)
           + _dotb(jnp.where(lower, gk[CH:2 * CH], 0.0), u['vp'])
           for u, arw, gb, gk, pm in zip(units, arws, gbs, gks, pms)]
    for u, wst, pm in zip(units, wsts, pms):
        w_sc[u['s'], u['h']] = u['decay'] * wst + _dotb_tn(u['lhs'], jnp.concatenate([pm, u['vp']], axis=0))
    for s, (p_ref, _, _, wo_ref, sho_ref) in enumerate(slots):
        r, k2, v, gg = tails[s]
        yh = yhs[s * R_HEADS:(s + 1) * R_HEADS]
        y = jnp.concatenate([jnp.where(lane < 64, yh[2 * pr], yh[2 * pr + 1]) for pr in range(R_HEADS // 2)], axis=1)
        mean = _dot_sel_rhs(y, bones) * (1.0 / R_HEAD)
        d = y - mean
        var = _dot_sel_rhs(d * d, bones) * (1.0 / R_HEAD)
        yn = d * lax.rsqrt(var + R_GN_EPS) * lnw_ref[...] + lnb_ref[...]
        bonus = _dot_sel_rhs(r * k2 * rk_ref[...], bones) * v
        _write_y(g, npc, yp_ref, ys_ref, s, (yn + bonus) * gg)
        last_row = jnp.where(g < npc, p_ref[CH - 1:CH, :], p_ref[t_s - 1:t_s, :])
        sh_sc[s] = jnp.broadcast_to(last_row, (8, GW))

        @pl.when(last)
        def _():
            wo_ref[...] = w_sc[s]
            sho_ref[...] = sh_sc[s]


def _head_block_ones(width, head):
    hid = np.arange(width) // head
    return jnp.asarray((hid[:, None] == hid[None, :]).astype(np.float32))


def _rwkv_consts(mu, w0, w2, a0, a2, g2, k_k, k_a, r_k, ln_w, ln_b):
    depth = mu.shape[0]

    def lora(w, off):
        return jnp.zeros((depth, LANES, GROUP), F32).at[:, off:off + R_LORA].set(w)

    def row(t):
        return t.reshape(depth, 1, GROUP)

    mu_p = jnp.zeros((depth, 1, GW), F32).at[:, 0, :P_R].set(mu)
    return [mu_p, row(w0), lora(w2, 0), row(a0), lora(a2, R_LORA), lora(g2, 2 * R_LORA), row(k_k), row(k_a),
            row(r_k), row(ln_w), row(ln_b)]


def _rwkv(p3, w0s, sh0, consts, layer, *, npc, n_sample, t_s):
    blocks = [(R_HEADS, 128, 128), (8, GW)]
    p_specs, in_state, out_state, y_specs = _mixer_specs(npc, 1, blocks, layer)
    y_shapes, st_shapes = _mixer_out_shapes(npc, n_sample, blocks)
    bones = _head_block_ones(GROUP, R_HEAD)
    return pl.pallas_call(
        functools.partial(_rwkv_kernel, npc=npc, t_s=t_s),
        grid=(npc + n_sample // 2,),
        in_specs=p_specs + in_state + [_layer_spec(c.shape[1:], layer) for c in consts]
        + [pl.BlockSpec(bones.shape, lambda g: (0, 0))],
        out_specs=y_specs + out_state,
        out_shape=y_shapes + st_shapes,
        scratch_shapes=[pltpu.VMEM((2, R_HEADS, 128, 128), F32), pltpu.VMEM((2, 8, GW), F32)],
        compiler_params=_cparams(("arbitrary",)),
        name="rwkv",
    )(p3, p3, w0s, w0s, sh0, sh0, *consts, bones)


def _moba_prep_kernel(p_ref, cos_ref, sin_ref, qn_ref, kn_ref, bones_ref, q_ref, k_ref, v_ref, kb_ref, vt_ref, km_ref):
    bones = bones_ref[...]
    cos = jnp.concatenate([cos_ref[...]] * 4, axis=1)
    sin = jnp.concatenate([sin_ref[...]] * 4, axis=1)
    lane = _iota(cos.shape, 1)
    low = (lane % A_HEAD) < (A_ROT // 2)

    def norm_rope(x, w):
        ms = _dot_sel_rhs(x * x, bones) * (1.0 / A_HEAD)
        xn = x * lax.rsqrt(ms + RMS_EPS) * w
        partner = jnp.where(low, pltpu.roll(xn, GROUP - A_ROT // 2, 1), pltpu.roll(xn, A_ROT // 2, 1))
        return xn * cos + partner * sin

    q_ref[...] = norm_rope(p_ref[:, 0:512], qn_ref[...])
    kr = norm_rope(p_ref[:, 512:1024], kn_ref[...])
    k_ref[...] = kr
    kb_ref[...] = kr.astype(BF16)
    v = p_ref[:, 1024:1536]
    v_ref[...] = v
    vt_ref[...] = v.T.astype(BF16)
    km_ref[...] = jnp.mean(kr, axis=0, keepdims=True)


def _moba_prep(p, cos_t, sin_t, q_norm, k_norm):
    rows = p.shape[0]
    nt = rows // MOBA_BLOCK
    rspec = pl.BlockSpec((MOBA_BLOCK, GROUP), lambda i: (i, 0))
    tspec = pl.BlockSpec((MOBA_BLOCK, LANES), lambda i: (i, 0))
    cspec = pl.BlockSpec((1, GROUP), lambda i: (0, 0))
    return pl.pallas_call(
        _moba_prep_kernel,
        grid=(nt,),
        in_specs=[pl.BlockSpec((MOBA_BLOCK, GW), lambda i: (i, 3)), tspec, tspec, cspec, cspec,
                  pl.BlockSpec((GROUP, GROUP), lambda i: (0, 0))],
        out_specs=[rspec, rspec, rspec, rspec, pl.BlockSpec((GROUP, MOBA_BLOCK), lambda i: (0, i)),
                   pl.BlockSpec((None, 1, GROUP), lambda i: (i, 0, 0))],
        out_shape=[jax.ShapeDtypeStruct((rows, GROUP), F32)] * 3
        + [jax.ShapeDtypeStruct((rows, GROUP), BF16), jax.ShapeDtypeStruct((GROUP, rows), BF16),
           jax.ShapeDtypeStruct((nt, 1, GROUP), F32)],
        compiler_params=_cparams(("parallel",)),
        name="moba_prep",
    )(p, cos_t, sin_t, q_norm, k_norm, _head_block_ones(GROUP, A_HEAD))


def _moba_prompt_kernel(q_ref, kb_ref, vt_ref, km_ref, y_ref, *, nb):
    i = pl.program_id(1)
    blk = MOBA_BLOCK
    scale = A_HEAD ** -0.5
    lane = _iota((blk, LANES), 1)
    lane_km = _iota((nb, LANES), 1)
    bidx = _iota((nb, blk), 0)
    causal = _iota((blk, blk), 0) <= _iota((blk, blk), 1)
    row0 = pl.multiple_of(i * blk, blk)
    prs = range(A_HEADS // 2)
    k_own = [kb_ref[pl.ds(row0, blk), 128 * pr:128 * pr + 128] for pr in prs]
    vt_own = [vt_ref[128 * pr:128 * pr + 128, pl.ds(row0, blk)] for pr in prs]
    hp = [(pr, half) for pr in prs for half in range(2)]
    qbs = [(jnp.where((lane // 64) == half, q_ref[:, 128 * pr:128 * pr + 128], 0.0) * scale).astype(BF16)
           for pr, half in hp]
    gates = [_dot_nt(jnp.where((lane_km // 64) == half, km_ref[:, 128 * pr:128 * pr + 128], 0.0),
                     q_ref[:, 128 * pr:128 * pr + 128], HI) for pr, half in hp]
    s_owns = [_dot_nt(k_own[pr], qb) for (pr, _), qb in zip(hp, qbs)]
    heads, p0s = [], []
    for (pr, _), qb, gate, s_own in zip(hp, qbs, gates, s_owns):
        gate = jnp.where(bidx < i, gate, -jnp.inf)
        picks = []
        for _ in range(MOBA_TOPK):
            mx = jnp.max(gate, axis=0, keepdims=True)
            idx = jnp.min(jnp.where(gate == mx, bidx, nb), axis=0, keepdims=True)
            picks.append(jnp.where(mx > -jnp.inf, idx, -1))
            gate = jnp.where(bidx == idx, -jnp.inf, gate)
        s_own = jnp.where(causal, s_own, NEG)
        m0 = jnp.max(s_own, axis=0, keepdims=True)
        p0 = jnp.exp(s_own - m0)
        p0s.append(p0.astype(BF16))
        heads.append(dict(pair=pr, qb=qb, picks=picks, m0=m0, l0=jnp.sum(p0, axis=0, keepdims=True)))
    acc0s = [_dot(vt_own[hd['pair']], p0) for hd, p0 in zip(heads, p0s)]
    inits = tuple((hd['m0'], hd['l0'], acc0) for hd, acc0 in zip(heads, acc0s))

    def body(j, carry):
        c0 = pl.multiple_of(j * blk, blk)
        kjs = [kb_ref[pl.ds(c0, blk), 128 * pr:128 * pr + 128] for pr in range(A_HEADS // 2)]
        vtjs = [vt_ref[128 * pr:128 * pr + 128, pl.ds(c0, blk)] for pr in range(A_HEADS // 2)]
        scores = [_dot_nt(kjs[hd['pair']], hd['qb']) for hd in heads]
        sts = []
        for hd, (m, l, acc), sc in zip(heads, carry, scores):
            picks = hd['picks']
            sel = jnp.logical_or(jnp.logical_or(picks[0] == j, picks[1] == j), picks[2] == j)
            sj = jnp.where(sel, sc, NEG)
            m_new = jnp.maximum(m, jnp.max(sj, axis=0, keepdims=True))
            alpha = jnp.exp(m - m_new)
            pj = jnp.exp(sj - m_new)
            sts.append((m_new, alpha, alpha * l + jnp.sum(pj, axis=0, keepdims=True), pj.astype(BF16)))
        pvs = [_dot(vtjs[hd['pair']], st[3]) for hd, st in zip(heads, sts)]
        return tuple((st[0], st[2], st[1] * acc + pv) for st, (_, _, acc), pv in zip(sts, carry, pvs))

    final = lax.fori_loop(0, i, body, inits)
    outs = [acc / l for _, l, acc in final]
    for pair in range(A_HEADS // 2):
        both = jnp.concatenate([outs[2 * pair][0:64], outs[2 * pair + 1][64:128]], axis=0)
        y_ref[:, 128 * pair:128 * pair + 128] = both.T


def _moba_prompt(q, kb, vt, km, *, nbatch, t):
    nb = t // MOBA_BLOCK
    return pl.pallas_call(
        functools.partial(_moba_prompt_kernel, nb=nb),
        grid=(nbatch, nb),
        in_specs=[pl.BlockSpec((MOBA_BLOCK, GROUP), lambda b, i: (b * nb + i, 0)),
                  pl.BlockSpec((t, GROUP), lambda b, i: (b, 0)),
                  pl.BlockSpec((GROUP, t), lambda b, i: (0, b)),
                  pl.BlockSpec((None, nb, GROUP), lambda b, i: (b, 0, 0))],
        out_specs=pl.BlockSpec((MOBA_BLOCK, GROUP), lambda b, i: (b * nb + i, 0)),
        out_shape=jax.ShapeDtypeStruct((nbatch * t, GROUP), F32),
        compiler_params=_cparams(("parallel", "arbitrary")),
        name="moba_prompt",
    )(q, kb, vt, km)


def _moba_gate_kernel(pt_ref, *refs, npages_step, nblocks, page):
    del pt_ref
    pg_refs = refs[:npages_step]
    qsel_ref, sel_ref, g_sc = refs[npages_step:]
    st = pl.program_id(1)
    per = npages_step // 2
    nq = qsel_ref.shape[0]
    q_hi, q_lo = _split2(qsel_ref[...])
    for i in range(per):
        part = (pg_refs[2 * i][...] + pg_refs[2 * i + 1][...]).reshape(A_HEADS * A_HEAD, page)
        p_hi, p_lo = _split2(part)
        g_sc[st * per + i] = _dot(q_hi, p_hi) + _dot(q_hi, p_lo) + _dot(q_lo, p_hi)

    @pl.when(st == pl.num_programs(1) - 1)
    def _():
        nidx = _iota((nblocks, nq, 1), 0)
        gate = jnp.sum(g_sc[...], axis=-1, keepdims=True) * (1.0 / (2 * page))
        for slot in range(MOBA_TOPK):
            mx = jnp.max(gate, axis=0, keepdims=True)
            idx = jnp.min(jnp.where(gate == mx, nidx, nblocks), axis=0, keepdims=True)
            for qi in range(nq // A_HEADS):
                sel_ref[qi * MOBA_TOPK + slot] = jnp.broadcast_to(idx[0, qi * A_HEADS:(qi + 1) * A_HEADS, :],
                                                                  (A_HEADS, LANES))
            gate = jnp.where(nidx == idx, -jnp.inf, gate)


def _moba_gate(cache_kt, layer, page_table, qsel):
    nseq, npages = page_table.shape
    page = cache_kt.shape[-1]
    nblocks = npages * page // MOBA_BLOCK
    npages_step = 8
    nq = qsel.shape[1]
    tq = nq // A_HEADS

    def pg_spec(i):
        return pl.BlockSpec((None, None, A_HEADS, A_HEAD, page),
                            lambda b, st, pt: (layer, pt[b, st * npages_step + i], 0, 0, 0))

    return pl.pallas_call(
        functools.partial(_moba_gate_kernel, npages_step=npages_step, nblocks=nblocks, page=page),
        grid_spec=pltpu.PrefetchScalarGridSpec(
            num_scalar_prefetch=1,
            grid=(nseq, npages // npages_step),
            in_specs=[pg_spec(i) for i in range(npages_step)]
            + [pl.BlockSpec((None, nq, A_HEADS * A_HEAD), lambda b, st, pt: (b, 0, 0))],
            out_specs=pl.BlockSpec((None, tq * MOBA_TOPK, A_HEADS, LANES), lambda b, st, pt: (b, 0, 0, 0)),
            scratch_shapes=[pltpu.VMEM((nblocks, nq, page), F32)]),
        out_shape=jax.ShapeDtypeStruct((nseq, tq * MOBA_TOPK, A_HEADS, LANES), jnp.int32),
        compiler_params=_cparams(("parallel", "arbitrary")),
        name="moba_gate",
    )(page_table, *([cache_kt] * npages_step), qsel)


def _moba_sample_kernel(pg_ref, q_ref, kn_ref, vn_ref, ck_ref, cv_ref, o_ref, kbuf, vbuf, sem, *, layer, tq, page):
    nh = pl.num_programs(1)
    step = pl.program_id(0) * nh + pl.program_id(1)
    total = pl.num_programs(0) * nh
    nslab = tq * MOBA_TOPK * (MOBA_BLOCK // page)

    def copies(st, c):
        slot, head = st % 2, st % nh
        pg = pg_ref[st * nslab + c]
        dst = pl.ds(c * page, page)
        return (pltpu.make_async_copy(ck_ref.at[layer, pg, head], kbuf.at[slot, :, dst], sem.at[slot, 0]),
                pltpu.make_async_copy(cv_ref.at[layer, pg, head], vbuf.at[slot, :, dst], sem.at[slot, 1]))

    def start_all(st):
        for c in range(nslab):
            for cp in copies(st, c):
                cp.start()

    @pl.when(step == 0)
    def _():
        start_all(step)

    @pl.when(step + 1 < total)
    def _():
        start_all(step + 1)

    for c in range(nslab):
        for cp in copies(step, c):
            cp.wait()
    slot = step % 2
    scale = A_HEAD ** -0.5
    nk = nslab * page
    q = q_ref[...]
    rows = q.shape[0]
    owner = _iota((rows, nk), 1) // (MOBA_TOPK * MOBA_BLOCK)
    s_sel = jnp.where(owner == _iota((rows, nk), 0), _dotb(q, kbuf[slot]) * scale, NEG)
    r8, c8 = _iota((rows, rows), 0), _iota((rows, rows), 1)
    s_own = jnp.where(jnp.logical_and(c8 <= r8, c8 < tq), _dotb_nt(q, kn_ref[...]) * scale, NEG)
    m = jnp.maximum(jnp.max(s_sel, axis=1, keepdims=True), jnp.max(s_own, axis=1, keepdims=True))
    p_sel = jnp.exp(s_sel - m)
    p_own = jnp.exp(s_own - m)
    l = jnp.sum(p_sel, axis=1, keepdims=True) + jnp.sum(p_own, axis=1, keepdims=True)
    o_ref[...] = (_dotb_nt(p_sel, vbuf[slot]) + _dotb(p_own, vn_ref[...])) / l


def _moba_sample(pages, q_s, kn_s, vn_s, cache_kt, cache_vt, layer, *, tq):
    nseq, nh, rows, hd = q_s.shape
    page = cache_kt.shape[-1]
    nslab = tq * MOBA_TOPK * (MOBA_BLOCK // page)
    spec = pl.BlockSpec((None, None, rows, hd), lambda b, h, pg: (b, h, 0, 0))
    return pl.pallas_call(
        functools.partial(_moba_sample_kernel, layer=layer, tq=tq, page=page),
        grid_spec=pltpu.PrefetchScalarGridSpec(
            num_scalar_prefetch=1,
            grid=(nseq, nh),
            in_specs=[spec, spec, spec, pl.BlockSpec(memory_space=pl.ANY), pl.BlockSpec(memory_space=pl.ANY)],
            out_specs=spec,
            scratch_shapes=[pltpu.VMEM((2, hd, nslab * page), F32), pltpu.VMEM((2, hd, nslab * page), F32),
                            pltpu.SemaphoreType.DMA((2, 2))]),
        out_shape=jax.ShapeDtypeStruct((nseq, nh, rows, hd), F32),
        compiler_params=_cparams(("arbitrary", "arbitrary")),
        name="moba_sample",
    )(pages, q_s, kn_s, vn_s, cache_kt, cache_vt)


def _pack_w_in(w):
    def dz(n):
        return jnp.zeros(w.shape[:2] + (n,), w.dtype)

    o_r, o_g, o_a = P_M, P_M + P_R, P_M + P_R + P_G
    kd = 2 * G_HEADS * G_DK + GROUP
    parts = [w[..., 0:P_M], dz(GW - P_M),
             w[..., o_r:o_r + P_R], dz(GW - P_R),
             w[..., o_g:o_g + kd], w[..., o_g + kd + G_LORA:o_g + P_G], w[..., o_g + kd:o_g + kd + G_LORA],
             dz(GW - P_G),
             w[..., o_a:o_a + P_A], dz(GW - P_A)]
    return jnp.concatenate(parts, axis=-1).astype(BF16)


def _even_head(nheads):
    return (jnp.arange(nheads) % 2 == 0).reshape((nheads, 1, 1))


def _place_half(x, nheads, axis):
    z = jnp.zeros_like(x)
    even = _even_head(nheads)
    return jnp.concatenate([jnp.where(even, x, z), jnp.where(even, z, x)], axis=axis)


def _take_half(x, nheads, axis):
    even = _even_head(nheads)
    lo = lax.slice_in_dim(x, 0, 64, axis=x.ndim + axis)
    hi = lax.slice_in_dim(x, 64, 128, axis=x.ndim + axis)
    return jnp.where(even, lo, hi)


def kernel(x_prompt, x_sample, cache_k, cache_v, state_mlstm_c, state_mlstm_n, state_mlstm_m, state_rwkv, state_rwkv_shift, state_gla, page_table, ffn1_norm, ffn1_w_in, ffn1_w_out, mix_norm, w_in, w_out, ffn2_norm, ffn2_w_in, ffn2_w_out, mlstm_gate_b, mlstm_norm_w, rwkv_mu, rwkv_w0, rwkv_w2, rwkv_a0, rwkv_a2, rwkv_g2, rwkv_k_k, rwkv_k_a, rwkv_r_k, rwkv_ln_w, rwkv_ln_b, gla_gk_up, gla_gk_b, gla_norm_w, moba_q_norm, moba_k_norm):
    nb, t, d = x_prompt.shape
    ns, ts, _ = x_sample.shape
    depth = w_in.shape[0]
    assert nb == 2 and ns % 2 == 0 and t % MOBA_BLOCK == 0 and ts <= 8
    npc = t // CH
    n_prompt = nb * t
    rows = n_prompt + ns * CH
    page = cache_k.shape[2]
    past_len = page_table.shape[1] * page
    cache_kt = jnp.transpose(cache_k, (0, 1, 3, 4, 2))
    cache_vt = jnp.transpose(cache_v, (0, 1, 3, 4, 2))
    kw = dict(npc=npc, n_sample=ns, t_s=ts)
    tm = 512 if rows % 512 == 0 else 256

    x = jnp.concatenate([x_prompt.reshape(n_prompt, d),
                         jnp.pad(x_sample, ((0, 0), (0, CH - ts), (0, 0))).reshape(ns * CH, d)], axis=0)
    pos = jnp.concatenate([jnp.tile(jnp.arange(t), nb), jnp.tile(past_len + jnp.arange(CH), ns)])
    inv = ROPE_THETA ** (-(jnp.arange(A_ROT // 2, dtype=F32) * 2.0 / A_ROT))
    ang = pos.astype(F32)[:, None] * inv[None, :]
    one = jnp.ones((rows, A_HEAD - A_ROT), F32)
    cos_h = jnp.concatenate([jnp.cos(ang), jnp.cos(ang), one], axis=1)
    sin_h = jnp.concatenate([-jnp.sin(ang), jnp.sin(ang), 0.0 * one], axis=1)
    cos_t = jnp.concatenate([cos_h, cos_h], axis=1)
    sin_t = jnp.concatenate([sin_h, sin_h], axis=1)

    w_in_p = _pack_w_in(w_in)
    w_out_b = w_out.astype(BF16)
    f1_in, f1_out = ffn1_w_in.astype(BF16), ffn1_w_out.astype(BF16)
    f2_in, f2_out = ffn2_w_in.astype(BF16), ffn2_w_out.astype(BF16)

    def with_prompt_zeros(st):
        return jnp.concatenate([jnp.zeros((depth, nb) + st.shape[2:], F32), st], axis=1)

    def sample_rows(a, n):
        return a[n_prompt:].reshape(ns, CH, a.shape[-1])[:, :n]

    def heads_major(a):
        return a.reshape(ns, 8, A_HEADS, A_HEAD).transpose(0, 2, 1, 3)

    ce = jnp.concatenate([state_mlstm_c, state_mlstm_n[..., None],
                          jnp.zeros(state_mlstm_n.shape + (LANES - 1,), F32)], axis=-1)
    ce0 = with_prompt_zeros(_place_half(ce, M_HEADS, -2))
    m0 = jnp.zeros((depth, nb + ns, 8, LANES), F32).at[:, nb:, 0, :M_HEADS].set(state_mlstm_m)
    gate_b = jnp.zeros((depth, 1, LANES), F32).at[:, 0, :2 * M_HEADS].set(mlstm_gate_b)
    w0s = with_prompt_zeros(_place_half(_place_half(jnp.swapaxes(state_rwkv, -1, -2), R_HEADS, -2), R_HEADS, -1))
    sh0 = jnp.zeros((depth, nb + ns, 8, GW), F32).at[:, nb:, :, :P_R].set(state_rwkv_shift[:, :, None, :])
    r_consts = _rwkv_consts(rwkv_mu, rwkv_w0, rwkv_w2, rwkv_a0, rwkv_a2, rwkv_g2, rwkv_k_k, rwkv_k_a,
                            rwkv_r_k.reshape(depth, GROUP), rwkv_ln_w, rwkv_ln_b)
    s0 = with_prompt_zeros(_place_half(state_gla, G_HEADS, -2))
    gk_up = jnp.zeros((depth, LANES, G_HEADS * G_DK), F32).at[:, :G_LORA].set(gla_gk_up)
    q_norm = jnp.tile(moba_q_norm, (1, A_HEADS)).reshape(depth, 1, GROUP)
    k_norm = jnp.tile(moba_k_norm, (1, A_HEADS)).reshape(depth, 1, GROUP)
    nbk = t // MOBA_BLOCK
    per_blk = MOBA_BLOCK // page

    raw = []
    for l in range(depth):
        x = _ffn(x, ffn1_norm[l], f1_in, f1_out, l, tm=tm, n_prompt=n_prompt)
        p = _proj(x, mix_norm[l], w_in_p, l, tm=tm, n_prompt=n_prompt)
        p3 = p.reshape(rows // CH, CH, NPK)
        ym_p, ym_s, ce_a, ce_b, m_a, m_b = _mlstm(p3, ce0, m0, gate_b, mlstm_norm_w.reshape(depth, 1, GROUP), l, **kw)
        yr_p, yr_s, w_a, w_b, sh_a, sh_b = _rwkv(p3, w0s, sh0, r_consts, l, **kw)
        yg_p, yg_s, g_a, g_b = _gla(p3, s0, gk_up, gla_gk_b.reshape(depth, 1, -1),
                                    gla_norm_w.reshape(depth, 1, G_DV), l, **kw)

        q_all, k_all, v_all, kb, vt, km = _moba_prep(p, cos_t, sin_t, q_norm[l], k_norm[l])
        ya_p = _moba_prompt(q_all, kb, vt, km[:nb * nbk].reshape(nb, nbk, GROUP), nbatch=nb, t=t)
        q_s = sample_rows(q_all, 8)
        q_heads = q_s[:, :ts].reshape(ns, ts, A_HEADS, 1, A_HEAD)
        qsel = jnp.where(jnp.eye(A_HEADS, dtype=bool)[:, :, None], q_heads, 0.0).reshape(ns, ts * A_HEADS, GROUP)
        sel = _moba_gate(cache_kt, l, page_table, qsel)[..., 0]
        sel = sel.reshape(ns, ts, MOBA_TOPK, A_HEADS).transpose(0, 3, 1, 2)
        pidx = per_blk * sel[..., None] + jnp.arange(per_blk)
        pages = page_table[jnp.arange(ns).reshape(ns, 1, 1, 1, 1), pidx].reshape(-1).astype(jnp.int32)
        o_s = _moba_sample(pages, heads_major(q_s), heads_major(sample_rows(k_all, 8)),
                           heads_major(sample_rows(v_all, 8)), cache_kt, cache_vt, l, tq=ts)
        ya_s = jnp.pad(o_s.transpose(0, 2, 1, 3).reshape(ns, 8, GROUP), ((0, 0), (0, CH - 8), (0, 0)))

        x = _outproj(x, [y.reshape(n_prompt, GROUP) for y in (ym_p, yr_p, yg_p, ya_p)],
                     [y.reshape(ns * CH, GROUP) for y in (ym_s, yr_s, yg_s, ya_s)], w_out_b, l, tm=tm)
        x = _ffn(x, ffn2_norm[l], f2_in, f2_out, l, tm=tm, n_prompt=n_prompt)
        raw.append((k_all, v_all, ce_a, ce_b, m_a, m_b, w_a, w_b, sh_a, sh_b, g_a, g_b))

    k_all, v_all, ce_a, ce_b, m_a, m_b, w_a, w_b, sh_a, sh_b, g_a, g_b = (
        jnp.stack([lay[i] for lay in raw]) for i in range(len(raw[0])))
    shp_p, shp_s = (depth, nb, t, A_HEADS, A_HEAD), (depth, ns, ts, A_HEADS, A_HEAD)

    def kv_sample(a):
        return a[:, n_prompt:].reshape(depth, ns, CH, GROUP)[:, :, :ts].reshape(shp_s)

    ce_p, ce_s = (_take_half(c, M_HEADS, -2) for c in _merge_state(ce_a, ce_b, nb))
    mm_p, mm_s = (m[:, :, 0, :M_HEADS] for m in _merge_state(m_a, m_b, nb))
    rw_p, rw_s = (jnp.swapaxes(_take_half(_take_half(w, R_HEADS, -2), R_HEADS, -1), -1, -2)
                  for w in _merge_state(w_a, w_b, nb))
    sh_p, sh_s = (sh[:, :, 0, :P_R] for sh in _merge_state(sh_a, sh_b, nb))
    gl_p, gl_s = (_take_half(gs, G_HEADS, -2) for gs in _merge_state(g_a, g_b, nb))
    y_p = x[:n_prompt].reshape(nb, t, d)
    y_s = x[n_prompt:].reshape(ns, CH, d)[:, :ts]
    return (y_p, y_s, k_all[:, :n_prompt].reshape(shp_p), v_all[:, :n_prompt].reshape(shp_p),
            kv_sample(k_all), kv_sample(v_all),
            ce_p[..., :M_DV], ce_s[..., :M_DV], ce_p[..., M_DV], ce_s[..., M_DV], mm_p, mm_s,
            rw_p, rw_s, sh_p, sh_s, gl_p, gl_s)
```

```python
import functools

import jax
import jax.numpy as jnp
import numpy as np
from jax import lax
from jax.experimental import pallas as pl
from jax.experimental.pallas import tpu as pltpu

F32 = jnp.float32
BF16 = jnp.bfloat16
HI = lax.Precision.HIGHEST

D_MODEL = 2048
GROUP = D_MODEL // 4
M_HEADS = 4
M_DV = GROUP // M_HEADS
M_DK = M_DV // 2
GATE_CAP = 15.0
R_HEAD = 64
R_HEADS = GROUP // R_HEAD
R_LORA = 32
R_GN_EPS = 64e-5
G_HEADS = 4
G_DV = GROUP // G_HEADS
G_DK = G_DV // 2
G_LORA = 16
G_TAU = 16.0
A_HEAD = 64
A_HEADS = GROUP // A_HEAD
A_ROT = A_HEAD // 4
ROPE_THETA = 500000.0
MOBA_BLOCK = 256
MOBA_TOPK = 3
RMS_EPS = 1e-6
P_M = 2 * M_HEADS * M_DK + 2 * GROUP + 2 * M_HEADS
P_R = 3 * GROUP + 3 * R_LORA
P_G = 2 * G_HEADS * G_DK + 2 * GROUP + G_LORA
P_A = 3 * GROUP

CH = 64
GW = 1664
NPK = 4 * GW
NEG = -1e30
LANES = 128
VMEM_LIMIT = 48 * 1024 * 1024


def _dot(a, b, prec=None):
    return jnp.dot(a, b, preferred_element_type=F32, precision=prec)


def _dot_nt(a, b, prec=None):
    return lax.dot_general(a, b, (((1,), (1,)), ((), ())), preferred_element_type=F32, precision=prec)


def _dot_tn(a, b, prec=None):
    return lax.dot_general(a, b, (((0,), (0,)), ((), ())), preferred_element_type=F32, precision=prec)


def _bf(x):
    return x.astype(BF16)


def _dotb(a, b):
    return _dot(_bf(a), _bf(b))


def _dotb_nt(a, b):
    return _dot_nt(_bf(a), _bf(b))


def _dotb_tn(a, b):
    return _dot_tn(_bf(a), _bf(b))


def _split2(x):
    hi = x.astype(BF16)
    return hi, (x - hi.astype(F32)).astype(BF16)


def _dot_sel_rhs(a, sel):
    hi, lo = _split2(a)
    sel = _bf(sel)
    return _dot(hi, sel) + _dot(lo, sel)


def _dot_sel_lhs(sel, b):
    hi, lo = _split2(b)
    sel = _bf(sel)
    return _dot(sel, hi) + _dot(sel, lo)


def _row_to_col(row, eye):
    return jnp.sum(eye * row, axis=1, keepdims=True)


def _iota(shape, dim):
    return lax.broadcasted_iota(jnp.int32, shape, dim)


def _sigmoid(x):
    return 1.0 / (1.0 + jnp.exp(-x))


def _softplus(x):
    return jnp.maximum(x, 0.0) + jnp.log(1.0 + jnp.exp(-jnp.abs(x)))


def _log_sigmoid(x):
    return -_softplus(-x)


def _tril(n):
    return (_iota((n, n), 1) <= _iota((n, n), 0)).astype(F32)


def _cparams(sem, vmem=VMEM_LIMIT):
    return pltpu.CompilerParams(dimension_semantics=sem, vmem_limit_bytes=vmem)


def _rms_rows(x, w):
    return x * lax.rsqrt(jnp.mean(x * x, axis=-1, keepdims=True) + RMS_EPS) * w


def _sample_rows(ref, rows_per_slot):
    tm, c = ref.shape
    return ref[...].reshape(tm // CH, CH, c)[:, 0:rows_per_slot, :].reshape(tm // CH * rows_per_slot, c)


def _scatter_sample_rows(o_ref, vals, rows_per_slot):
    for b in range(o_ref.shape[0] // CH):
        o_ref[CH * b:CH * b + rows_per_slot, :] = vals[rows_per_slot * b:rows_per_slot * (b + 1), :]


def _ffn_kernel(x_ref, nw_ref, wg_ref, wu_ref, wo_ref, o_ref, h_ref, acc_ref, *, prompt_tiles, ms):
    i, j = pl.program_id(0), pl.program_id(1)
    last = j == pl.num_programs(1) - 1

    def swiglu_part(h):
        gate = _dot(h, wg_ref[...])
        up = _dot(h, wu_ref[...])
        return _dot((gate * _sigmoid(gate) * up).astype(BF16), wo_ref[...])

    @pl.when(i < prompt_tiles)
    def _():
        @pl.when(j == 0)
        def _():
            h_ref[...] = _rms_rows(x_ref[...], nw_ref[...]).astype(BF16)
            acc_ref[...] = jnp.zeros_like(acc_ref)

        acc_ref[...] += swiglu_part(h_ref[...])

        @pl.when(last)
        def _():
            o_ref[...] = x_ref[...] + 0.5 * acc_ref[...]

    @pl.when(i >= prompt_tiles)
    def _():
        @pl.when(j == 0)
        def _():
            h_ref[0:ms, :] = _rms_rows(_sample_rows(x_ref, 8), nw_ref[...]).astype(BF16)
            acc_ref[0:ms, :] = jnp.zeros((ms, acc_ref.shape[1]), F32)

        acc_ref[0:ms, :] += swiglu_part(h_ref[0:ms, :])

        @pl.when(last)
        def _():
            o_ref[...] = x_ref[...]
            _scatter_sample_rows(o_ref, _sample_rows(x_ref, 8) + 0.5 * acc_ref[0:ms, :], 8)


def _ffn(x, norm_w, w_in, w_out, layer, *, tm, n_prompt, tf=512):
    rows, d = x.shape
    dff = w_out.shape[1]
    nj = dff // tf
    return pl.pallas_call(
        functools.partial(_ffn_kernel, prompt_tiles=n_prompt // tm, ms=tm // CH * 8),
        grid=(rows // tm, nj),
        in_specs=[
            pl.BlockSpec((tm, d), lambda i, j: (i, 0)),
            pl.BlockSpec((1, d), lambda i, j: (0, 0)),
            pl.BlockSpec((None, d, tf), lambda i, j: (layer, 0, j)),
            pl.BlockSpec((None, d, tf), lambda i, j: (layer, 0, j + nj)),
            pl.BlockSpec((None, tf, d), lambda i, j: (layer, j, 0)),
        ],
        out_specs=pl.BlockSpec((tm, d), lambda i, j: (i, 0)),
        out_shape=jax.ShapeDtypeStruct((rows, d), F32),
        scratch_shapes=[pltpu.VMEM((tm, d), BF16), pltpu.VMEM((tm, d), F32)],
        compiler_params=_cparams(("parallel", "arbitrary")),
        name="ffn",
    )(x, norm_w.reshape(1, d), w_in, w_in, w_out)


def _proj_kernel(x_ref, nw_ref, w_ref, o_ref, h_ref, *, prompt_tiles, ms):
    i, j = pl.program_id(0), pl.program_id(1)

    @pl.when(i < prompt_tiles)
    def _():
        @pl.when(j == 0)
        def _():
            h_ref[...] = _rms_rows(x_ref[...], nw_ref[...]).astype(BF16)

        o_ref[...] = _dot(h_ref[...], w_ref[...])

    @pl.when(i >= prompt_tiles)
    def _():
        @pl.when(j == 0)
        def _():
            h_ref[0:ms, :] = _rms_rows(_sample_rows(x_ref, 8), nw_ref[...]).astype(BF16)

        o_ref[...] = jnp.zeros_like(o_ref)
        _scatter_sample_rows(o_ref, _dot(h_ref[0:ms, :], w_ref[...]), 8)


def _proj(x, norm_w, w, layer, *, tm, n_prompt, tn=GW):
    rows, d = x.shape
    n = w.shape[2]
    return pl.pallas_call(
        functools.partial(_proj_kernel, prompt_tiles=n_prompt // tm, ms=tm // CH * 8),
        grid=(rows // tm, n // tn),
        in_specs=[
            pl.BlockSpec((tm, d), lambda i, j: (i, 0)),
            pl.BlockSpec((1, d), lambda i, j: (0, 0)),
            pl.BlockSpec((None, d, tn), lambda i, j: (layer, 0, j)),
        ],
        out_specs=pl.BlockSpec((tm, tn), lambda i, j: (i, j)),
        out_shape=jax.ShapeDtypeStruct((rows, n), F32),
        scratch_shapes=[pltpu.VMEM((tm, d), BF16)],
        compiler_params=_cparams(("parallel", "arbitrary")),
        name="proj",
    )(x, norm_w.reshape(1, d), w)


def _outproj_kernel(x_ref, *refs, prompt_tiles):
    yp_refs, ys_refs, w_ref, o_ref = refs[0:4], refs[4:8], refs[8], refs[9]

    def run(y_refs):
        acc = x_ref[...]
        for gi, y_ref in enumerate(y_refs):
            acc = acc + _dot(y_ref[...].astype(BF16), w_ref[gi * GROUP:(gi + 1) * GROUP, :])
        o_ref[...] = acc

    @pl.when(pl.program_id(0) < prompt_tiles)
    def _():
        run(yp_refs)

    @pl.when(pl.program_id(0) >= prompt_tiles)
    def _():
        run(ys_refs)


def _outproj(x, y_prompt, y_sample, w, layer, *, tm):
    rows, d = x.shape
    pt = y_prompt[0].shape[0] // tm
    p_spec = pl.BlockSpec((tm, GROUP), lambda i: (jnp.minimum(i, pt - 1), 0))
    s_spec = pl.BlockSpec((tm, GROUP), lambda i: (jnp.maximum(i - pt, 0), 0))
    return pl.pallas_call(
        functools.partial(_outproj_kernel, prompt_tiles=pt),
        grid=(rows // tm,),
        in_specs=[pl.BlockSpec((tm, d), lambda i: (i, 0))] + [p_spec] * 4 + [s_spec] * 4
        + [pl.BlockSpec((None, d, d), lambda i: (layer, 0, 0))],
        out_specs=pl.BlockSpec((tm, d), lambda i: (i, 0)),
        out_shape=jax.ShapeDtypeStruct((rows, d), F32),
        compiler_params=_cparams(("parallel",)),
        name="outproj",
    )(x, *y_prompt, *y_sample, w)


def _layer_spec(shape, layer):
    return pl.BlockSpec((None,) + shape, lambda g: (layer,) + (0,) * len(shape))


def _mixer_specs(npc, group, state_blocks, layer):
    def chunk_idx(s):
        return lambda g: (jnp.where(g < npc, s * npc + g, 2 * npc + 2 * (g - npc) + s), 0, group)

    p_specs = [pl.BlockSpec((None, CH, GW), chunk_idx(s)) for s in (0, 1)]
    in_state, out_state = [], []
    for blk in state_blocks:
        zeros = (0,) * len(blk)
        for s in (0, 1):
            in_state.append(pl.BlockSpec(
                (None, None) + blk,
                lambda g, s=s, z=zeros: (layer, jnp.where(g < npc, s, 2 + 2 * (g - npc) + s)) + z))
            out_state.append(pl.BlockSpec(
                (None,) + blk, lambda g, z=zeros: (jnp.where(g < npc, 0, 1 + g - npc),) + z))
    y_specs = [pl.BlockSpec((2, None, CH, GROUP), lambda g: (0, jnp.minimum(g, npc - 1), 0, 0)),
               pl.BlockSpec((None, 2, CH, GROUP), lambda g: (jnp.maximum(g - npc, 0), 0, 0, 0))]
    return p_specs, in_state, out_state, y_specs


def _mixer_out_shapes(npc, n_sample, state_blocks):
    ys = [jax.ShapeDtypeStruct((2, npc, CH, GROUP), F32), jax.ShapeDtypeStruct((n_sample // 2, 2, CH, GROUP), F32)]
    st = []
    for blk in state_blocks:
        st += [jax.ShapeDtypeStruct((1 + n_sample // 2,) + blk, F32)] * 2
    return ys, st


def _step_info(npc, t_s):
    g = pl.program_id(0)
    first = jnp.logical_or(g == 0, g >= npc)
    last = g >= npc - 1
    tlen = jnp.where(g < npc, CH, t_s)
    return g, first, last, tlen


def _write_y(g, npc, yp_ref, ys_ref, s, val):
    @pl.when(g < npc)
    def _():
        yp_ref[s] = val

    @pl.when(g >= npc)
    def _():
        ys_ref[s] = val


def _merge_state(a, b, nb):
    assert nb == 2
    prompt = jnp.stack([a[:, 0], b[:, 0]], axis=1)
    sample = jnp.stack([a[:, 1:], b[:, 1:]], axis=2).reshape((a.shape[0], -1) + a.shape[2:])
    return prompt, sample


def _mlstm_kernel(pa_ref, pb_ref, cea_ref, ceb_ref, ma_ref, mb_ref, gb_ref, nw_ref,
                  yp_ref, ys_ref, ceoa_ref, ceob_ref, moa_ref, mob_ref, ce_sc, m_sc, *, npc, t_s):
    g, first, last, tlen = _step_info(npc, t_s)
    slots = ((pa_ref, cea_ref, ma_ref, ceoa_ref, moa_ref), (pb_ref, ceb_ref, mb_ref, ceob_ref, mob_ref))
    row = _iota((CH, LANES), 0)
    lane = _iota((CH, LANES), 1)
    tril = _tril(CH)
    causal = _iota((CH, CH), 1) <= _iota((CH, CH), 0)
    e0 = (lane == 0).astype(F32)
    valid = row < tlen

    @pl.when(first)
    def _():
        for s, (_, ce0_ref, m0_ref, _, _) in enumerate(slots):
            ce_sc[s] = ce0_ref[...]
            m_sc[s] = m0_ref[...]

    units = []
    for s, (p_ref, _, _, _, _) in enumerate(slots):
        gates = p_ref[:, 1536:1664] + gb_ref[...]
        gates = GATE_CAP * jnp.tanh(gates / GATE_CAP)
        ig = jnp.where(valid, gates, NEG)
        lf = jnp.where(valid, _log_sigmoid(gates), 0.0)
        b_col = _dot_sel_lhs(tril, lf)
        ig_t = ig.T
        lf_hi, lf_lo = _split2(lf.T[0:8])
        b_row = _dot_nt(lf_hi, _bf(tril)) + _dot_nt(lf_lo, _bf(tril))
        for h in range(M_HEADS):
            pair, half = h // 2, h % 2
            hm = (lane // 64) == half
            b_c = b_col[:, 4 + h:5 + h]
            m_h = m_sc[s, 0:1, h:h + 1]
            dmat = jnp.where(causal, b_c - b_row[4 + h:5 + h, :] + ig_t[h:h + 1, :], NEG)
            m_inter = b_c + m_h
            mt = jnp.maximum(m_inter, jnp.max(dmat, axis=1, keepdims=True))
            m_new = mt[CH - 1:CH, :]
            b_last = b_c[CH - 1:CH, :]
            units.append(dict(
                s=s, h=h, p_ref=p_ref, mt=mt, m_new=m_new, dexp=jnp.exp(dmat - mt), s_inter=jnp.exp(m_inter - mt),
                carry=jnp.exp(b_last + m_h - m_new), ws=jnp.exp(b_last - b_c + ig[:, h:h + 1] - m_new),
                qm=jnp.where(hm, p_ref[:, 128 * pair:128 * pair + 128], 0.0),
                km=jnp.where(hm, p_ref[:, 256 + 128 * pair:256 + 128 * pair + 128], 0.0) * (M_DK ** -0.5),
                v_ext=jnp.concatenate([p_ref[:, 512 + 128 * h:512 + 128 * h + 128], e0], axis=1)))
    ces = [ce_sc[u['s'], u['h']] for u in units]
    qks = [_dotb_nt(u['qm'], u['km']) for u in units]
    qcs = [_dotb(u['qm'], ce) for u, ce in zip(units, ces)]
    nds = [u['s_inter'] * qc + _dotb(u['dexp'] * qk, u['v_ext']) for u, qk, qc in zip(units, qks, qcs)]
    for u, ce in zip(units, ces):
        ce_sc[u['s'], u['h']] = u['carry'] * ce + _dotb_tn(u['km'], u['ws'] * u['v_ext'])
        m_sc[u['s'], 0:1, u['h']:u['h'] + 1] = u['m_new']
    for s, (p_ref, _, _, ceo_ref, mo_ref) in enumerate(slots):
        ys = []
        for h in range(M_HEADS):
            u, nd = units[s * M_HEADS + h], nds[s * M_HEADS + h]
            hh = nd[:, 0:128] / jnp.maximum(jnp.abs(nd[:, 128:129]), jnp.exp(-u['mt']))
            hn = hh * lax.rsqrt(jnp.mean(hh * hh, axis=-1, keepdims=True) + RMS_EPS)
            hn = hn * nw_ref[:, 128 * h:128 * h + 128]
            ys.append(hn * _sigmoid(p_ref[:, 1024 + 128 * h:1024 + 128 * h + 128]))
        _write_y(g, npc, yp_ref, ys_ref, s, jnp.concatenate(ys, axis=1))

        @pl.when(last)
        def _():
            ceo_ref[...] = ce_sc[s]
            mo_ref[...] = m_sc[s]


def _mlstm(p3, ce0, m0, gate_b, norm_w, layer, *, npc, n_sample, t_s):
    blocks = [(M_HEADS, 128, 256), (8, LANES)]
    p_specs, in_state, out_state, y_specs = _mixer_specs(npc, 0, blocks, layer)
    y_shapes, st_shapes = _mixer_out_shapes(npc, n_sample, blocks)
    return pl.pallas_call(
        functools.partial(_mlstm_kernel, npc=npc, t_s=t_s),
        grid=(npc + n_sample // 2,),
        in_specs=p_specs + in_state + [_layer_spec((1, LANES), layer), _layer_spec((1, GROUP), layer)],
        out_specs=y_specs + out_state,
        out_shape=y_shapes + st_shapes,
        scratch_shapes=[pltpu.VMEM((2, M_HEADS, 128, 256), F32), pltpu.VMEM((2, 8, LANES), F32)],
        compiler_params=_cparams(("arbitrary",)),
        name="mlstm",
    )(p3, p3, ce0, ce0, m0, m0, gate_b, norm_w)


def _gla_kernel(pa_ref, pb_ref, sa_ref, sb_ref, up_ref, gkb_ref, nw_ref,
                yp_ref, ys_ref, soa_ref, sob_ref, s_sc, k_sc, b_sc, *, npc, t_s):
    g, first, last, tlen = _step_info(npc, t_s)
    slots = ((pa_ref, sa_ref, soa_ref), (pb_ref, sb_ref, sob_ref))
    row = _iota((CH, LANES), 0)
    lane = _iota((CH, LANES), 1)
    tril = _tril(CH)
    r128 = _iota((LANES, LANES), 0)
    c128 = _iota((LANES, LANES), 1)
    half_ones = ((r128 // 64) == (c128 // 64)).astype(F32)
    eye = (r128 == c128).astype(F32)
    valid = row < tlen

    @pl.when(first)
    def _():
        for s, (_, s0_ref, _) in enumerate(slots):
            s_sc[s] = s0_ref[...]

    pairs = []
    for s, (p_ref, _, _) in enumerate(slots):
        z = _dotb(p_ref[:, 1536:1664], up_ref[...]) + gkb_ref[...]
        la = _log_sigmoid(z) / G_TAU
        la = jnp.where(jnp.concatenate([valid, valid], axis=1), la, 0.0)
        b = _dot_sel_lhs(tril, la)
        for pair in range(2):
            sl = slice(128 * pair, 128 * pair + 128)
            kp = jnp.where(valid, p_ref[:, 256 + 128 * pair:256 + 128 * pair + 128], 0.0)
            bp = b[:, sl]
            k_sc[s, pair] = kp
            b_sc[s, pair] = bp
            pairs.append(dict(s=s, pair=pair, p_ref=p_ref, qp=p_ref[:, sl] * (G_DK ** -0.5), kp=kp, bp=bp,
                              bl=bp[CH - 1:CH, :]))

    def body(ti, accs):
        r0 = pl.multiple_of(ti * 8, 8)
        sums = []
        for u in pairs:
            k8 = k_sc[u['s'], u['pair'], pl.ds(r0, 8), :]
            b8 = b_sc[u['s'], u['pair'], pl.ds(r0, 8), :]
            e = [u['qp'] * k8[j:j + 1, :] * jnp.exp(jnp.minimum(u['bp'] - b8[j:j + 1, :], 0.0)) for j in range(8)]
            sums.append(_dotb(jnp.concatenate(e, axis=0), half_ones))
        out = []
        for acc, sm in zip(accs, sums):
            for j in range(8):
                acc = jnp.where((lane % 64) == r0 + j, sm[CH * j:CH * (j + 1)], acc)
            out.append(acc)
        return tuple(out)

    atts = lax.fori_loop(0, CH // 8, body, tuple(jnp.zeros((CH, LANES), F32) for _ in pairs))
    units = []
    for u, att in zip(pairs, atts):
        att = jnp.where((lane % 64) <= row, att, 0.0)
        decay = _row_to_col(jnp.exp(u['bl']), eye)
        for half in range(2):
            h = 2 * u['pair'] + half
            hm = (lane // 64) == half
            v_h = u['p_ref'][:, 512 + 128 * h:512 + 128 * h + 128]
            units.append(dict(s=u['s'], h=h, v_h=v_h, decay=decay, p_ref=u['p_ref'],
                              qe=jnp.where(hm, u['qp'] * jnp.exp(u['bp']), 0.0), att=jnp.where(hm, att, 0.0),
                              khat=jnp.where(hm, u['kp'] * jnp.exp(u['bl'] - u['bp']), 0.0)))
    sts = [s_sc[u['s'], u['h']] for u in units]
    outs = [_dotb(u['qe'], st) + _dotb(u['att'], jnp.concatenate([u['v_h'], u['v_h']], axis=0))
            for u, st in zip(units, sts)]
    for u, st in zip(units, sts):
        s_sc[u['s'], u['h']] = u['decay'] * st + _dotb_tn(u['khat'], u['v_h'])
    for s, (p_ref, _, so_ref) in enumerate(slots):
        ys = []
        for h in range(G_HEADS):
            o = outs[s * G_HEADS + h]
            on = o * lax.rsqrt(jnp.mean(o * o, axis=-1, keepdims=True) + RMS_EPS) * nw_ref[...]
            gt = p_ref[:, 1024 + 128 * h:1024 + 128 * h + 128]
            ys.append(on * gt * _sigmoid(gt))
        _write_y(g, npc, yp_ref, ys_ref, s, jnp.concatenate(ys, axis=1))

        @pl.when(last)
        def _():
            so_ref[...] = s_sc[s]


def _gla(p3, s0, gk_up, gk_b, norm_w, layer, *, npc, n_sample, t_s):
    blocks = [(G_HEADS, 128, 128)]
    p_specs, in_state, out_state, y_specs = _mixer_specs(npc, 2, blocks, layer)
    y_shapes, st_shapes = _mixer_out_shapes(npc, n_sample, blocks)
    return pl.pallas_call(
        functools.partial(_gla_kernel, npc=npc, t_s=t_s),
        grid=(npc + n_sample // 2,),
        in_specs=p_specs + in_state
        + [_layer_spec((LANES, 256), layer), _layer_spec((1, 256), layer), _layer_spec((1, G_DV), layer)],
        out_specs=y_specs + out_state,
        out_shape=y_shapes + st_shapes,
        scratch_shapes=[pltpu.VMEM((2, G_HEADS, 128, 128), F32), pltpu.VMEM((2, 2, CH, LANES), F32),
                        pltpu.VMEM((2, 2, CH, LANES), F32)],
        compiler_params=_cparams(("arbitrary",)),
        name="gla",
    )(p3, p3, s0, s0, gk_up, gk_b, norm_w)


def _rwkv_kernel(pa_ref, pb_ref, wa_ref, wb_ref, sha_ref, shb_ref, mu_ref, w0_ref, w2_ref, a0_ref, a2_ref,
                 g2_ref, kk_ref, ka_ref, rk_ref, lnw_ref, lnb_ref, bones_ref,
                 yp_ref, ys_ref, woa_ref, wob_ref, shoa_ref, shob_ref, w_sc, sh_sc, *, npc, t_s):
    g, first, last, tlen = _step_info(npc, t_s)
    slots = ((pa_ref, wa_ref, sha_ref, woa_ref, shoa_ref), (pb_ref, wb_ref, shb_ref, wob_ref, shob_ref))
    lane = _iota((CH, LANES), 1)
    tril = _tril(CH)
    r64, c64 = _iota((CH, CH), 0), _iota((CH, CH), 1)
    lower_strict = c64 < r64
    lower = c64 <= r64
    eye64 = (r64 == c64).astype(F32)
    eye128 = (_iota((LANES, LANES), 0) == _iota((LANES, LANES), 1)).astype(F32)
    bones = bones_ref[...]
    valid = _iota((CH, GROUP), 0) < tlen

    @pl.when(first)
    def _():
        for s, (_, w0s_ref, sh0_ref, _, _) in enumerate(slots):
            w_sc[s] = w0s_ref[...]
            sh_sc[s] = sh0_ref[...]

    units, tails = [], []
    for s, (p_ref, _, _, _, _) in enumerate(slots):
        pf = p_ref[...]
        prev = jnp.where(_iota((CH, GW), 0) == 0, sh_sc[s, 0:1, :], pltpu.roll(pf, 1, 0))
        xs = pf + (prev - pf) * mu_ref[...]
        r = xs[:, 0:512]
        k = xs[:, 512:1024]
        v = xs[:, 1024:1536]
        lo = xs[:, 1536:1664]
        wraw = -_softplus(-(w0_ref[...] + _dotb(jnp.tanh(lo), w2_ref[...]))) - 0.5
        logw = jnp.where(valid, -jnp.exp(wraw), 0.0)
        a = _sigmoid(a0_ref[...] + _dotb(lo, a2_ref[...]))
        gg = _dotb(_sigmoid(lo), g2_ref[...])
        kk = k * kk_ref[...]
        kk = kk / jnp.maximum(jnp.sqrt(_dot_sel_rhs(kk * kk, bones)), 1e-12)
        k2 = k * (1.0 + (a - 1.0) * ka_ref[...])
        k2m = jnp.where(valid, k2, 0.0)
        bv = jnp.where(valid, kk * a, 0.0)
        lg = _dot_sel_lhs(tril, logw)
        lgl = lg[CH - 1:CH, :]
        e_out = jnp.exp(-lg)
        e_end = jnp.exp(lgl - lg)
        at = -kk * jnp.exp(lg - logw)
        rt = r * jnp.exp(lg)
        bt = bv * e_out
        kt = k2m * e_out
        bh = bv * e_end
        kh = k2m * e_end
        gl = jnp.exp(lgl)
        tails.append((r, k2, v, gg))
        for pair in range(R_HEADS // 2):
            sl = slice(128 * pair, 128 * pair + 128)
            decay = _row_to_col(gl[:, sl], eye128)
            for half in range(2):
                hm = (lane // 64) == half
                units.append(dict(
                    s=s, h=2 * pair + half, vp=v[:, sl], decay=decay, bt=bt[:, sl], kt=kt[:, sl],
                    ar=jnp.concatenate([jnp.where(hm, at[:, sl], 0.0), jnp.where(hm, rt[:, sl], 0.0)], axis=0),
                    lhs=jnp.concatenate([jnp.where(hm, bh[:, sl], 0.0), jnp.where(hm, kh[:, sl], 0.0)], axis=0)))
    gbs = [_dotb_nt(u['ar'], u['bt']) for u in units]
    gks = [_dotb_nt(u['ar'], u['kt']) for u in units]
    nmat = [jnp.where(lower_strict, gb[0:CH], 0.0) for gb in gbs]
    tms = [eye64 + n for n in nmat]
    mxs = [_dotb(n, n) for n in nmat]
    for it in range(5):
        prods = [_dotb(jnp.concatenate([tm, mx], axis=0), mx) for tm, mx in zip(tms, mxs)]
        tms = [tm + pr[0:CH] for tm, pr in zip(tms, prods)]
        if it < 4:
            mxs = [pr[CH:2 * CH] for pr in prods]
    wsts = [w_sc[u['s'], u['h']] for u in units]
    arws = [_dotb(u['ar'], wst) for u, wst in zip(units, wsts)]
    inner = [_dotb(jnp.where(lower_strict, gk[0:CH], 0.0), u['vp']) for u, gk in zip(units, gks)]
    pms = [_dotb(tm, arw[0:CH] + inn) for tm, arw, inn in zip(tms, arws, inner)]
    yhs = [arw[CH:2 * CH] + _dotb(jnp.where(lower, gb[CH:2 * CH], 0.0), pm)
           + _dotb(jnp.where(lower, gk[CH:2 * CH], 0.0), u['vp'])
           for u, arw, gb, gk, pm in zip(units, arws, gbs, gks, pms)]
    for u, wst, pm in zip(units, wsts, pms):
        w_sc[u['s'], u['h']] = u['decay'] * wst + _dotb_tn(u['lhs'], jnp.concatenate([pm, u['vp']], axis=0))
    for s, (p_ref, _, _, wo_ref, sho_ref) in enumerate(slots):
        r, k2, v, gg = tails[s]
        yh = yhs[s * R_HEADS:(s + 1) * R_HEADS]
        y = jnp.concatenate([jnp.where(lane < 64, yh[2 * pr], yh[2 * pr + 1]) for pr in range(R_HEADS // 2)], axis=1)
        mean = _dot_sel_rhs(y, bones) * (1.0 / R_HEAD)
        d = y - mean
        var = _dot_sel_rhs(d * d, bones) * (1.0 / R_HEAD)
        yn = d * lax.rsqrt(var + R_GN_EPS) * lnw_ref[...] + lnb_ref[...]
        bonus = _dot_sel_rhs(r * k2 * rk_ref[...], bones) * v
        _write_y(g, npc, yp_ref, ys_ref, s, (yn + bonus) * gg)
        last_row = jnp.where(g < npc, p_ref[CH - 1:CH, :], p_ref[t_s - 1:t_s, :])
        sh_sc[s] = jnp.broadcast_to(last_row, (8, GW))

        @pl.when(last)
        def _():
            wo_ref[...] = w_sc[s]
            sho_ref[...] = sh_sc[s]


def _head_block_ones(width, head):
    hid = np.arange(width) // head
    return jnp.asarray((hid[:, None] == hid[None, :]).astype(np.float32))


def _rwkv_consts(mu, w0, w2, a0, a2, g2, k_k, k_a, r_k, ln_w, ln_b):
    depth = mu.shape[0]

    def lora(w, off):
        return jnp.zeros((depth, LANES, GROUP), F32).at[:, off:off + R_LORA].set(w)

    def row(t):
        return t.reshape(depth, 1, GROUP)

    mu_p = jnp.zeros((depth, 1, GW), F32).at[:, 0, :P_R].set(mu)
    return [mu_p, row(w0), lora(w2, 0), row(a0), lora(a2, R_LORA), lora(g2, 2 * R_LORA), row(k_k), row(k_a),
            row(r_k), row(ln_w), row(ln_b)]


def _rwkv(p3, w0s, sh0, consts, layer, *, npc, n_sample, t_s):
    blocks = [(R_HEADS, 128, 128), (8, GW)]
    p_specs, in_state, out_state, y_specs = _mixer_specs(npc, 1, blocks, layer)
    y_shapes, st_shapes = _mixer_out_shapes(npc, n_sample, blocks)
    bones = _head_block_ones(GROUP, R_HEAD)
    return pl.pallas_call(
        functools.partial(_rwkv_kernel, npc=npc, t_s=t_s),
        grid=(npc + n_sample // 2,),
        in_specs=p_specs + in_state + [_layer_spec(c.shape[1:], layer) for c in consts]
        + [pl.BlockSpec(bones.shape, lambda g: (0, 0))],
        out_specs=y_specs + out_state,
        out_shape=y_shapes + st_shapes,
        scratch_shapes=[pltpu.VMEM((2, R_HEADS, 128, 128), F32), pltpu.VMEM((2, 8, GW), F32)],
        compiler_params=_cparams(("arbitrary",)),
        name="rwkv",
    )(p3, p3, w0s, w0s, sh0, sh0, *consts, bones)


def _moba_prep_kernel(p_ref, cos_ref, sin_ref, qn_ref, kn_ref, bones_ref, q_ref, k_ref, v_ref, kb_ref, vt_ref, km_ref):
    bones = bones_ref[...]
    cos = jnp.concatenate([cos_ref[...]] * 4, axis=1)
    sin = jnp.concatenate([sin_ref[...]] * 4, axis=1)
    lane = _iota(cos.shape, 1)
    low = (lane % A_HEAD) < (A_ROT // 2)

    def norm_rope(x, w):
        ms = _dot_sel_rhs(x * x, bones) * (1.0 / A_HEAD)
        xn = x * lax.rsqrt(ms + RMS_EPS) * w
        partner = jnp.where(low, pltpu.roll(xn, GROUP - A_ROT // 2, 1), pltpu.roll(xn, A_ROT // 2, 1))
        return xn * cos + partner * sin

    q_ref[...] = norm_rope(p_ref[:, 0:512], qn_ref[...])
    kr = norm_rope(p_ref[:, 512:1024], kn_ref[...])
    k_ref[...] = kr
    kb_ref[...] = kr.astype(BF16)
    v = p_ref[:, 1024:1536]
    v_ref[...] = v
    vt_ref[...] = v.T.astype(BF16)
    km_ref[...] = jnp.mean(kr, axis=0, keepdims=True)


def _moba_prep(p, cos_t, sin_t, q_norm, k_norm):
    rows = p.shape[0]
    nt = rows // MOBA_BLOCK
    rspec = pl.BlockSpec((MOBA_BLOCK, GROUP), lambda i: (i, 0))
    tspec = pl.BlockSpec((MOBA_BLOCK, LANES), lambda i: (i, 0))
    cspec = pl.BlockSpec((1, GROUP), lambda i: (0, 0))
    return pl.pallas_call(
        _moba_prep_kernel,
        grid=(nt,),
        in_specs=[pl.BlockSpec((MOBA_BLOCK, GW), lambda i: (i, 3)), tspec, tspec, cspec, cspec,
                  pl.BlockSpec((GROUP, GROUP), lambda i: (0, 0))],
        out_specs=[rspec, rspec, rspec, rspec, pl.BlockSpec((GROUP, MOBA_BLOCK), lambda i: (0, i)),
                   pl.BlockSpec((None, 1, GROUP), lambda i: (i, 0, 0))],
        out_shape=[jax.ShapeDtypeStruct((rows, GROUP), F32)] * 3
        + [jax.ShapeDtypeStruct((rows, GROUP), BF16), jax.ShapeDtypeStruct((GROUP, rows), BF16),
           jax.ShapeDtypeStruct((nt, 1, GROUP), F32)],
        compiler_params=_cparams(("parallel",)),
        name="moba_prep",
    )(p, cos_t, sin_t, q_norm, k_norm, _head_block_ones(GROUP, A_HEAD))


def _moba_prompt_kernel(q_ref, kb_ref, vt_ref, km_ref, y_ref, *, nb):
    i = pl.program_id(1)
    blk = MOBA_BLOCK
    scale = A_HEAD ** -0.5
    lane = _iota((blk, LANES), 1)
    lane_km = _iota((nb, LANES), 1)
    bidx = _iota((nb, blk), 0)
    causal = _iota((blk, blk), 0) <= _iota((blk, blk), 1)
    row0 = pl.multiple_of(i * blk, blk)
    prs = range(A_HEADS // 2)
    k_own = [kb_ref[pl.ds(row0, blk), 128 * pr:128 * pr + 128] for pr in prs]
    vt_own = [vt_ref[128 * pr:128 * pr + 128, pl.ds(row0, blk)] for pr in prs]
    hp = [(pr, half) for pr in prs for half in range(2)]
    qbs = [(jnp.where((lane // 64) == half, q_ref[:, 128 * pr:128 * pr + 128], 0.0) * scale).astype(BF16)
           for pr, half in hp]
    gates = [_dot_nt(jnp.where((lane_km // 64) == half, km_ref[:, 128 * pr:128 * pr + 128], 0.0),
                     q_ref[:, 128 * pr:128 * pr + 128], HI) for pr, half in hp]
    s_owns = [_dot_nt(k_own[pr], qb) for (pr, _), qb in zip(hp, qbs)]
    heads, p0s = [], []
    for (pr, _), qb, gate, s_own in zip(hp, qbs, gates, s_owns):
        gate = jnp.where(bidx < i, gate, -jnp.inf)
        picks = []
        for _ in range(MOBA_TOPK):
            mx = jnp.max(gate, axis=0, keepdims=True)
            idx = jnp.min(jnp.where(gate == mx, bidx, nb), axis=0, keepdims=True)
            picks.append(jnp.where(mx > -jnp.inf, idx, -1))
            gate = jnp.where(bidx == idx, -jnp.inf, gate)
        s_own = jnp.where(causal, s_own, NEG)
        m0 = jnp.max(s_own, axis=0, keepdims=True)
        p0 = jnp.exp(s_own - m0)
        p0s.append(p0.astype(BF16))
        heads.append(dict(pair=pr, qb=qb, picks=picks, m0=m0, l0=jnp.sum(p0, axis=0, keepdims=True)))
    acc0s = [_dot(vt_own[hd['pair']], p0) for hd, p0 in zip(heads, p0s)]
    inits = tuple((hd['m0'], hd['l0'], acc0) for hd, acc0 in zip(heads, acc0s))

    def body(j, carry):
        c0 = pl.multiple_of(j * blk, blk)
        kjs = [kb_ref[pl.ds(c0, blk), 128 * pr:128 * pr + 128] for pr in range(A_HEADS // 2)]
        vtjs = [vt_ref[128 * pr:128 * pr + 128, pl.ds(c0, blk)] for pr in range(A_HEADS // 2)]
        scores = [_dot_nt(kjs[hd['pair']], hd['qb']) for hd in heads]
        sts = []
        for hd, (m, l, acc), sc in zip(heads, carry, scores):
            picks = hd['picks']
            sel = jnp.logical_or(jnp.logical_or(picks[0] == j, picks[1] == j), picks[2] == j)
            sj = jnp.where(sel, sc, NEG)
            m_new = jnp.maximum(m, jnp.max(sj, axis=0, keepdims=True))
            alpha = jnp.exp(m - m_new)
            pj = jnp.exp(sj - m_new)
            sts.append((m_new, alpha, alpha * l + jnp.sum(pj, axis=0, keepdims=True), pj.astype(BF16)))
        pvs = [_dot(vtjs[hd['pair']], st[3]) for hd, st in zip(heads, sts)]
        return tuple((st[0], st[2], st[1] * acc + pv) for st, (_, _, acc), pv in zip(sts, carry, pvs))

    final = lax.fori_loop(0, i, body, inits)
    outs = [acc / l for _, l, acc in final]
    for pair in range(A_HEADS // 2):
        both = jnp.concatenate([outs[2 * pair][0:64], outs[2 * pair + 1][64:128]], axis=0)
        y_ref[:, 128 * pair:128 * pair + 128] = both.T


def _moba_prompt(q, kb, vt, km, *, nbatch, t):
    nb = t // MOBA_BLOCK
    return pl.pallas_call(
        functools.partial(_moba_prompt_kernel, nb=nb),
        grid=(nbatch, nb),
        in_specs=[pl.BlockSpec((MOBA_BLOCK, GROUP), lambda b, i: (b * nb + i, 0)),
                  pl.BlockSpec((t, GROUP), lambda b, i: (b, 0)),
                  pl.BlockSpec((GROUP, t), lambda b, i: (0, b)),
                  pl.BlockSpec((None, nb, GROUP), lambda b, i: (b, 0, 0))],
        out_specs=pl.BlockSpec((MOBA_BLOCK, GROUP), lambda b, i: (b * nb + i, 0)),
        out_shape=jax.ShapeDtypeStruct((nbatch * t, GROUP), F32),
        compiler_params=_cparams(("parallel", "arbitrary")),
        name="moba_prompt",
    )(q, kb, vt, km)


def _moba_gate_kernel(pt_ref, *refs, npages_step, nblocks, page):
    del pt_ref
    pg_refs = refs[:npages_step]
    qsel_ref, sel_ref, g_sc = refs[npages_step:]
    st = pl.program_id(1)
    per = npages_step // 2
    nq = qsel_ref.shape[0]
    q_hi, q_lo = _split2(qsel_ref[...])
    for i in range(per):
        part = (pg_refs[2 * i][...] + pg_refs[2 * i + 1][...]).reshape(A_HEADS * A_HEAD, page)
        p_hi, p_lo = _split2(part)
        g_sc[st * per + i] = _dot(q_hi, p_hi) + _dot(q_hi, p_lo) + _dot(q_lo, p_hi)

    @pl.when(st == pl.num_programs(1) - 1)
    def _():
        nidx = _iota((nblocks, nq, 1), 0)
        gate = jnp.sum(g_sc[...], axis=-1, keepdims=True) * (1.0 / (2 * page))
        for slot in range(MOBA_TOPK):
            mx = jnp.max(gate, axis=0, keepdims=True)
            idx = jnp.min(jnp.where(gate == mx, nidx, nblocks), axis=0, keepdims=True)
            for qi in range(nq // A_HEADS):
                sel_ref[qi * MOBA_TOPK + slot] = jnp.broadcast_to(idx[0, qi * A_HEADS:(qi + 1) * A_HEADS, :],
                                                                  (A_HEADS, LANES))
            gate = jnp.where(nidx == idx, -jnp.inf, gate)


def _moba_gate(cache_kt, layer, page_table, qsel):
    nseq, npages = page_table.shape
    page = cache_kt.shape[-1]
    nblocks = npages * page // MOBA_BLOCK
    npages_step = 16
    nq = qsel.shape[1]
    tq = nq // A_HEADS

    def pg_spec(i):
        return pl.BlockSpec((None, None, A_HEADS, A_HEAD, page),
                            lambda b, st, pt: (layer, pt[b, st * npages_step + i], 0, 0, 0))

    return pl.pallas_call(
        functools.partial(_moba_gate_kernel, npages_step=npages_step, nblocks=nblocks, page=page),
        grid_spec=pltpu.PrefetchScalarGridSpec(
            num_scalar_prefetch=1,
            grid=(nseq, npages // npages_step),
            in_specs=[pg_spec(i) for i in range(npages_step)]
            + [pl.BlockSpec((None, nq, A_HEADS * A_HEAD), lambda b, st, pt: (b, 0, 0))],
            out_specs=pl.BlockSpec((None, tq * MOBA_TOPK, A_HEADS, LANES), lambda b, st, pt: (b, 0, 0, 0)),
            scratch_shapes=[pltpu.VMEM((nblocks, nq, page), F32)]),
        out_shape=jax.ShapeDtypeStruct((nseq, tq * MOBA_TOPK, A_HEADS, LANES), jnp.int32),
        compiler_params=_cparams(("parallel", "arbitrary")),
        name="moba_gate",
    )(page_table, *([cache_kt] * npages_step), qsel)


def _moba_sample_kernel(pg_ref, q_ref, kn_ref, vn_ref, ck_ref, cv_ref, o_ref, kbuf, vbuf, sem, *, layer, tq, page):
    nh = pl.num_programs(1)
    step = pl.program_id(0) * nh + pl.program_id(1)
    total = pl.num_programs(0) * nh
    nslab = tq * MOBA_TOPK * (MOBA_BLOCK // page)

    def copies(st, c):
        slot, head = st % 2, st % nh
        pg = pg_ref[st * nslab + c]
        dst = pl.ds(c * page, page)
        return (pltpu.make_async_copy(ck_ref.at[layer, pg, head], kbuf.at[slot, :, dst], sem.at[slot, 0]),
                pltpu.make_async_copy(cv_ref.at[layer, pg, head], vbuf.at[slot, :, dst], sem.at[slot, 1]))

    def start_all(st):
        for c in range(nslab):
            for cp in copies(st, c):
                cp.start()

    @pl.when(step == 0)
    def _():
        start_all(step)

    @pl.when(step + 1 < total)
    def _():
        start_all(step + 1)

    for c in range(nslab):
        for cp in copies(step, c):
            cp.wait()
    slot = step % 2
    scale = A_HEAD ** -0.5
    nk = nslab * page
    q = q_ref[...]
    rows = q.shape[0]
    owner = _iota((rows, nk), 1) // (MOBA_TOPK * MOBA_BLOCK)
    s_sel = jnp.where(owner == _iota((rows, nk), 0), _dotb(q, kbuf[slot]) * scale, NEG)
    r8, c8 = _iota((rows, rows), 0), _iota((rows, rows), 1)
    s_own = jnp.where(jnp.logical_and(c8 <= r8, c8 < tq), _dotb_nt(q, kn_ref[...]) * scale, NEG)
    m = jnp.maximum(jnp.max(s_sel, axis=1, keepdims=True), jnp.max(s_own, axis=1, keepdims=True))
    p_sel = jnp.exp(s_sel - m)
    p_own = jnp.exp(s_own - m)
    l = jnp.sum(p_sel, axis=1, keepdims=True) + jnp.sum(p_own, axis=1, keepdims=True)
    o_ref[...] = (_dotb_nt(p_sel, vbuf[slot]) + _dotb(p_own, vn_ref[...])) / l


def _moba_sample(pages, q_s, kn_s, vn_s, cache_kt, cache_vt, layer, *, tq):
    nseq, nh, rows, hd = q_s.shape
    page = cache_kt.shape[-1]
    nslab = tq * MOBA_TOPK * (MOBA_BLOCK // page)
    spec = pl.BlockSpec((None, None, rows, hd), lambda b, h, pg: (b, h, 0, 0))
    return pl.pallas_call(
        functools.partial(_moba_sample_kernel, layer=layer, tq=tq, page=page),
        grid_spec=pltpu.PrefetchScalarGridSpec(
            num_scalar_prefetch=1,
            grid=(nseq, nh),
            in_specs=[spec, spec, spec, pl.BlockSpec(memory_space=pl.ANY), pl.BlockSpec(memory_space=pl.ANY)],
            out_specs=spec,
            scratch_shapes=[pltpu.VMEM((2, hd, nslab * page), F32), pltpu.VMEM((2, hd, nslab * page), F32),
                            pltpu.SemaphoreType.DMA((2, 2))]),
        out_shape=jax.ShapeDtypeStruct((nseq, nh, rows, hd), F32),
        compiler_params=_cparams(("arbitrary", "arbitrary")),
        name="moba_sample",
    )(pages, q_s, kn_s, vn_s, cache_kt, cache_vt)


def _pack_w_in(w):
    def dz(n):
        return jnp.zeros(w.shape[:2] + (n,), w.dtype)

    o_r, o_g, o_a = P_M, P_M + P_R, P_M + P_R + P_G
    kd = 2 * G_HEADS * G_DK + GROUP
    parts = [w[..., 0:P_M], dz(GW - P_M),
             w[..., o_r:o_r + P_R], dz(GW - P_R),
             w[..., o_g:o_g + kd], w[..., o_g + kd + G_LORA:o_g + P_G], w[..., o_g + kd:o_g + kd + G_LORA],
             dz(GW - P_G),
             w[..., o_a:o_a + P_A], dz(GW - P_A)]
    return jnp.concatenate(parts, axis=-1).astype(BF16)


def _even_head(nheads):
    return (jnp.arange(nheads) % 2 == 0).reshape((nheads, 1, 1))


def _place_half(x, nheads, axis):
    z = jnp.zeros_like(x)
    even = _even_head(nheads)
    return jnp.concatenate([jnp.where(even, x, z), jnp.where(even, z, x)], axis=axis)


def _take_half(x, nheads, axis):
    even = _even_head(nheads)
    lo = lax.slice_in_dim(x, 0, 64, axis=x.ndim + axis)
    hi = lax.slice_in_dim(x, 64, 128, axis=x.ndim + axis)
    return jnp.where(even, lo, hi)


def kernel(x_prompt, x_sample, cache_k, cache_v, state_mlstm_c, state_mlstm_n, state_mlstm_m, state_rwkv, state_rwkv_shift, state_gla, page_table, ffn1_norm, ffn1_w_in, ffn1_w_out, mix_norm, w_in, w_out, ffn2_norm, ffn2_w_in, ffn2_w_out, mlstm_gate_b, mlstm_norm_w, rwkv_mu, rwkv_w0, rwkv_w2, rwkv_a0, rwkv_a2, rwkv_g2, rwkv_k_k, rwkv_k_a, rwkv_r_k, rwkv_ln_w, rwkv_ln_b, gla_gk_up, gla_gk_b, gla_norm_w, moba_q_norm, moba_k_norm):
    nb, t, d = x_prompt.shape
    ns, ts, _ = x_sample.shape
    depth = w_in.shape[0]
    assert nb == 2 and ns % 2 == 0 and t % MOBA_BLOCK == 0 and ts <= 8
    npc = t // CH
    n_prompt = nb * t
    rows = n_prompt + ns * CH
    page = cache_k.shape[2]
    past_len = page_table.shape[1] * page
    cache_kt = jnp.transpose(cache_k, (0, 1, 3, 4, 2))
    cache_vt = jnp.transpose(cache_v, (0, 1, 3, 4, 2))
    kw = dict(npc=npc, n_sample=ns, t_s=ts)
    tm = 512 if rows % 512 == 0 else 256

    x = jnp.concatenate([x_prompt.reshape(n_prompt, d),
                         jnp.pad(x_sample, ((0, 0), (0, CH - ts), (0, 0))).reshape(ns * CH, d)], axis=0)
    pos = jnp.concatenate([jnp.tile(jnp.arange(t), nb), jnp.tile(past_len + jnp.arange(CH), ns)])
    inv = ROPE_THETA ** (-(jnp.arange(A_ROT // 2, dtype=F32) * 2.0 / A_ROT))
    ang = pos.astype(F32)[:, None] * inv[None, :]
    one = jnp.ones((rows, A_HEAD - A_ROT), F32)
    cos_h = jnp.concatenate([jnp.cos(ang), jnp.cos(ang), one], axis=1)
    sin_h = jnp.concatenate([-jnp.sin(ang), jnp.sin(ang), 0.0 * one], axis=1)
    cos_t = jnp.concatenate([cos_h, cos_h], axis=1)
    sin_t = jnp.concatenate([sin_h, sin_h], axis=1)

    w_in_p = _pack_w_in(w_in)
    w_out_b = w_out.astype(BF16)
    f1_in, f1_out = ffn1_w_in.astype(BF16), ffn1_w_out.astype(BF16)
    f2_in, f2_out = ffn2_w_in.astype(BF16), ffn2_w_out.astype(BF16)

    def with_prompt_zeros(st):
        return jnp.concatenate([jnp.zeros((depth, nb) + st.shape[2:], F32), st], axis=1)

    def sample_rows(a, n):
        return a[n_prompt:].reshape(ns, CH, a.shape[-1])[:, :n]

    def heads_major(a):
        return a.reshape(ns, 8, A_HEADS, A_HEAD).transpose(0, 2, 1, 3)

    ce = jnp.concatenate([state_mlstm_c, state_mlstm_n[..., None],
                          jnp.zeros(state_mlstm_n.shape + (LANES - 1,), F32)], axis=-1)
    ce0 = with_prompt_zeros(_place_half(ce, M_HEADS, -2))
    m0 = jnp.zeros((depth, nb + ns, 8, LANES), F32).at[:, nb:, 0, :M_HEADS].set(state_mlstm_m)
    gate_b = jnp.zeros((depth, 1, LANES), F32).at[:, 0, :2 * M_HEADS].set(mlstm_gate_b)
    w0s = with_prompt_zeros(_place_half(_place_half(jnp.swapaxes(state_rwkv, -1, -2), R_HEADS, -2), R_HEADS, -1))
    sh0 = jnp.zeros((depth, nb + ns, 8, GW), F32).at[:, nb:, :, :P_R].set(state_rwkv_shift[:, :, None, :])
    r_consts = _rwkv_consts(rwkv_mu, rwkv_w0, rwkv_w2, rwkv_a0, rwkv_a2, rwkv_g2, rwkv_k_k, rwkv_k_a,
                            rwkv_r_k.reshape(depth, GROUP), rwkv_ln_w, rwkv_ln_b)
    s0 = with_prompt_zeros(_place_half(state_gla, G_HEADS, -2))
    gk_up = jnp.zeros((depth, LANES, G_HEADS * G_DK), F32).at[:, :G_LORA].set(gla_gk_up)
    q_norm = jnp.tile(moba_q_norm, (1, A_HEADS)).reshape(depth, 1, GROUP)
    k_norm = jnp.tile(moba_k_norm, (1, A_HEADS)).reshape(depth, 1, GROUP)
    nbk = t // MOBA_BLOCK
    per_blk = MOBA_BLOCK // page

    raw = []
    for l in range(depth):
        x = _ffn(x, ffn1_norm[l], f1_in, f1_out, l, tm=tm, n_prompt=n_prompt)
        p = _proj(x, mix_norm[l], w_in_p, l, tm=tm, n_prompt=n_prompt)
        p3 = p.reshape(rows // CH, CH, NPK)
        ym_p, ym_s, ce_a, ce_b, m_a, m_b = _mlstm(p3, ce0, m0, gate_b, mlstm_norm_w.reshape(depth, 1, GROUP), l, **kw)
        yr_p, yr_s, w_a, w_b, sh_a, sh_b = _rwkv(p3, w0s, sh0, r_consts, l, **kw)
        yg_p, yg_s, g_a, g_b = _gla(p3, s0, gk_up, gla_gk_b.reshape(depth, 1, -1),
                                    gla_norm_w.reshape(depth, 1, G_DV), l, **kw)

        q_all, k_all, v_all, kb, vt, km = _moba_prep(p, cos_t, sin_t, q_norm[l], k_norm[l])
        ya_p = _moba_prompt(q_all, kb, vt, km[:nb * nbk].reshape(nb, nbk, GROUP), nbatch=nb, t=t)
        q_s = sample_rows(q_all, 8)
        q_heads = q_s[:, :ts].reshape(ns, ts, A_HEADS, 1, A_HEAD)
        qsel = jnp.where(jnp.eye(A_HEADS, dtype=bool)[:, :, None], q_heads, 0.0).reshape(ns, ts * A_HEADS, GROUP)
        sel = _moba_gate(cache_kt, l, page_table, qsel)[..., 0]
        sel = sel.reshape(ns, ts, MOBA_TOPK, A_HEADS).transpose(0, 3, 1, 2)
        pidx = per_blk * sel[..., None] + jnp.arange(per_blk)
        pages = page_table[jnp.arange(ns).reshape(ns, 1, 1, 1, 1), pidx].reshape(-1).astype(jnp.int32)
        o_s = _moba_sample(pages, heads_major(q_s), heads_major(sample_rows(k_all, 8)),
                           heads_major(sample_rows(v_all, 8)), cache_kt, cache_vt, l, tq=ts)
        ya_s = jnp.pad(o_s.transpose(0, 2, 1, 3).reshape(ns, 8, GROUP), ((0, 0), (0, CH - 8), (0, 0)))

        x = _outproj(x, [y.reshape(n_prompt, GROUP) for y in (ym_p, yr_p, yg_p, ya_p)],
                     [y.reshape(ns * CH, GROUP) for y in (ym_s, yr_s, yg_s, ya_s)], w_out_b, l, tm=tm)
        x = _ffn(x, ffn2_norm[l], f2_in, f2_out, l, tm=tm, n_prompt=n_prompt)
        raw.append((k_all, v_all, ce_a, ce_b, m_a, m_b, w_a, w_b, sh_a, sh_b, g_a, g_b))

    k_all, v_all, ce_a, ce_b, m_a, m_b, w_a, w_b, sh_a, sh_b, g_a, g_b = (
        jnp.stack([lay[i] for lay in raw]) for i in range(len(raw[0])))
    shp_p, shp_s = (depth, nb, t, A_HEADS, A_HEAD), (depth, ns, ts, A_HEADS, A_HEAD)

    def kv_sample(a):
        return a[:, n_prompt:].reshape(depth, ns, CH, GROUP)[:, :, :ts].reshape(shp_s)

    ce_p, ce_s = (_take_half(c, M_HEADS, -2) for c in _merge_state(ce_a, ce_b, nb))
    mm_p, mm_s = (m[:, :, 0, :M_HEADS] for m in _merge_state(m_a, m_b, nb))
    rw_p, rw_s = (jnp.swapaxes(_take_half(_take_half(w, R_HEADS, -2), R_HEADS, -1), -1, -2)
                  for w in _merge_state(w_a, w_b, nb))
    sh_p, sh_s = (sh[:, :, 0, :P_R] for sh in _merge_state(sh_a, sh_b, nb))
    gl_p, gl_s = (_take_half(gs, G_HEADS, -2) for gs in _merge_state(g_a, g_b, nb))
    y_p = x[:n_prompt].reshape(nb, t, d)
    y_s = x[n_prompt:].reshape(ns, CH, d)[:, :ts]
    return (y_p, y_s, k_all[:, :n_prompt].reshape(shp_p), v_all[:, :n_prompt].reshape(shp_p),
            kv_sample(k_all), kv_sample(v_all),
            ce_p[..., :M_DV], ce_s[..., :M_DV], ce_p[..., M_DV], ce_s[..., M_DV], mm_p, mm_s,
            rw_p, rw_s, sh_p, sh_s, gl_p, gl_s)
```
